```python
import jax, jax.numpy as jnp
from jax import lax
import numpy as np

D_MODEL = 2048
BATCH = 4
SEQ = 2048
DEPTH = 2
DEC_BATCH = 128
DEC_SEQ = 8
PAST_LEN = 16384
PAGE_SIZE = 128

HEAD_DIM = 128
POOL_WINDOWS = (2, 4, 8, 16)
N_POOL_GROUPS = len(POOL_WINDOWS)
POOL_WIDTH = D_MODEL // 4
POOL_GROUP_DIM = POOL_WIDTH // N_POOL_GROUPS
POOL_HIST = max(POOL_WINDOWS) - 1
CONV_WIDTH = 3 * D_MODEL // 8
N_CONV_HEADS = CONV_WIDTH // HEAD_DIM
SHORT_CONV = 3
SGU_WIDTH = 3 * D_MODEL // 8
N_SGU_HEADS = SGU_WIDTH // HEAD_DIM
SGU_HEAD_DIM = HEAD_DIM
CHUNK = 128
MIX_WIDTH = POOL_WIDTH + CONV_WIDTH + SGU_WIDTH
IN_WIDTH = POOL_WIDTH + 3 * CONV_WIDTH + 2 * SGU_WIDTH
D_FF = 11 * D_MODEL // 4
FFN_CONV = 3
EPS = 1e-6

kernel_name = "hybrid_pool_shortconv_chunkgmlp_decoder_step"


def rmsnorm(x, g):
    xf = x.astype(jnp.float32)
    y = xf * lax.rsqrt(jnp.mean(xf * xf, axis=-1, keepdims=True) + EPS)
    return (y * g.astype(jnp.float32)).astype(x.dtype)


def causal_conv(z, prev, w):
    k = w.shape[0]
    t = z.shape[1]
    zz = jnp.concatenate([prev, z], axis=1)
    y = sum(w[i] * zz[:, i:i + t] for i in range(k))
    return y, zz[:, -(k - 1):]


def pool_mixer(xa, prev, start, pool_w, pool_scale):
    b, t, p = xa.shape
    zz = jnp.concatenate([prev, xa], axis=1)
    cs = jnp.concatenate([jnp.zeros((b, 1, p), jnp.float32),
                          jnp.cumsum(zz.astype(jnp.float32), axis=1)], axis=1)
    pos = start + jnp.arange(t)
    lo = POOL_HIST + 1
    outs = []
    for g, w in enumerate(POOL_WINDOWS):
        sl = slice(g * POOL_GROUP_DIM, (g + 1) * POOL_GROUP_DIM)
        s = cs[:, lo:lo + t, sl] - cs[:, lo - w:lo - w + t, sl]
        cnt = jnp.minimum(w, pos + 1).astype(jnp.float32)
        outs.append(s / cnt[None, :, None])
    pooled = jnp.stack(outs, axis=2)
    pooled = pooled - xa.astype(jnp.float32).reshape(b, t, N_POOL_GROUPS, POOL_GROUP_DIM)
    y = jnp.einsum('btgc,gcd->btgd', pooled.astype(xa.dtype), pool_w).reshape(b, t, p)
    return y * pool_scale, zz[:, -POOL_HIST:]


def short_conv_mixer(h, bg, cg, prev, conv_w):
    y, new = causal_conv(cg * h, prev, conv_w)
    return bg * y, new


def sgu_mixer(u, v, sgu_norm, sgu_w, sgu_b):
    b, t, s = v.shape
    v = rmsnorm(v, sgu_norm)
    n_chunks = -(-t // CHUNK)
    pad = n_chunks * CHUNK - t
    vp = jnp.pad(v, ((0, 0), (0, pad), (0, 0))).reshape(b, n_chunks, CHUNK, N_SGU_HEADS, SGU_HEAD_DIM)
    mask = jnp.tril(jnp.ones((CHUNK, CHUNK), dtype=bool))
    ws = jnp.where(mask[None], sgu_w, jnp.zeros_like(sgu_w))
    g = jnp.einsum('hqs,bnshd->bnqhd', ws, vp) + sgu_b.T[None, None, :, :, None]
    g = g.reshape(b, n_chunks * CHUNK, s)[:, :t]
    return u * g, v


def conv_ffn(x, prev, up, cw, cb, down):
    a = x @ up
    gate, val = jnp.split(a, 2, axis=-1)
    gc, new = causal_conv(gate, prev, cw)
    h = jax.nn.gelu(gc + cb, approximate=True) * val
    return h @ down, new


def block(x, pool_prev, conv_prev, ffn_prev, start, g_mix_pre, g_mix_post, g_ffn_pre, g_ffn_post,
          w_in, pool_w, pool_scale, conv_w, sgu_norm, sgu_w, sgu_b, w_out,
          ffn_up, ffn_conv_w, ffn_conv_b, ffn_down):
    h = rmsnorm(x, g_mix_pre)
    p = h @ w_in
    o1 = POOL_WIDTH
    o2 = o1 + CONV_WIDTH
    o3 = o2 + CONV_WIDTH
    o4 = o3 + CONV_WIDTH
    o5 = o4 + SGU_WIDTH
    ya, new_pool = pool_mixer(p[..., :o1], pool_prev, start, pool_w, pool_scale)
    yb, new_conv = short_conv_mixer(p[..., o1:o2], p[..., o2:o3], p[..., o3:o4], conv_prev, conv_w)
    yc, v_rows = sgu_mixer(p[..., o4:o5], p[..., o5:], sgu_norm, sgu_w, sgu_b)
    mix = jnp.concatenate([ya, yb, yc], axis=-1) @ w_out
    x = x + rmsnorm(mix, g_mix_post)
    f, new_ffn = conv_ffn(rmsnorm(x, g_ffn_pre), ffn_prev, ffn_up, ffn_conv_w, ffn_conv_b, ffn_down)
    x = x + rmsnorm(f, g_ffn_post)
    return x, new_pool, new_conv, new_ffn, v_rows


def setup_inputs(seed: int = 0) -> dict:
    key = jax.random.key(seed)
    ks = jax.random.split(key, 24)
    f32 = jnp.float32
    nrm = lambda k, shape, scale: (jax.random.normal(k, shape, f32) * scale)
    gain = lambda k, shape: 1.0 + 0.05 * jax.random.normal(k, shape, f32)
    return {
        "x_prompt": nrm(ks[0], (BATCH, SEQ, D_MODEL), 1.0),
        "x_sample": nrm(ks[1], (DEC_BATCH, DEC_SEQ, D_MODEL), 1.0),
        "state_pool": nrm(ks[2], (DEPTH, DEC_BATCH, POOL_HIST, POOL_WIDTH), 1.0),
        "state_conv": nrm(ks[3], (DEPTH, DEC_BATCH, SHORT_CONV - 1, CONV_WIDTH), 1.0),
        "state_ffn": nrm(ks[4], (DEPTH, DEC_BATCH, FFN_CONV - 1, D_FF), 1.0),
        "g_mix_pre": gain(ks[5], (DEPTH, D_MODEL)),
        "g_mix_post": gain(ks[6], (DEPTH, D_MODEL)),
        "g_ffn_pre": gain(ks[7], (DEPTH, D_MODEL)),
        "g_ffn_post": gain(ks[8], (DEPTH, D_MODEL)),
        "w_in": nrm(ks[9], (DEPTH, D_MODEL, IN_WIDTH), D_MODEL ** -0.5),
        "pool_w": nrm(ks[10], (DEPTH, N_POOL_GROUPS, POOL_GROUP_DIM, POOL_GROUP_DIM), POOL_GROUP_DIM ** -0.5),
        "pool_scale": gain(ks[11], (DEPTH, POOL_WIDTH)),
        "conv_w": nrm(ks[12], (DEPTH, SHORT_CONV, CONV_WIDTH), SHORT_CONV ** -0.5),
        "sgu_norm": gain(ks[13], (DEPTH, SGU_WIDTH)),
        "sgu_w": nrm(ks[14], (DEPTH, N_SGU_HEADS, CHUNK, CHUNK), CHUNK ** -0.5),
        "sgu_b": gain(ks[15], (DEPTH, N_SGU_HEADS, CHUNK)),
        "w_out": nrm(ks[16], (DEPTH, MIX_WIDTH, D_MODEL), MIX_WIDTH ** -0.5),
        "ffn_up": nrm(ks[17], (DEPTH, D_MODEL, 2 * D_FF), D_MODEL ** -0.5),
        "ffn_conv_w": nrm(ks[18], (DEPTH, FFN_CONV, D_FF), FFN_CONV ** -0.5),
        "ffn_conv_b": nrm(ks[19], (DEPTH, D_FF), 0.02),
        "ffn_down": nrm(ks[20], (DEPTH, D_FF, D_MODEL), D_FF ** -0.5),
    }


def reference(x_prompt, x_sample, state_pool, state_conv, state_ffn,
              g_mix_pre, g_mix_post, g_ffn_pre, g_ffn_post, w_in, pool_w, pool_scale,
              conv_w, sgu_norm, sgu_w, sgu_b, w_out, ffn_up, ffn_conv_w, ffn_conv_b, ffn_down):
    bp = x_prompt.shape[0]
    dt = x_prompt.dtype
    zero_pool = jnp.zeros((bp, POOL_HIST, POOL_WIDTH), dt)
    zero_conv = jnp.zeros((bp, SHORT_CONV - 1, CONV_WIDTH), dt)
    zero_ffn = jnp.zeros((bp, FFN_CONV - 1, D_FF), dt)
    hp = x_prompt
    hs = x_sample
    pool_p, pool_s, conv_p, conv_s, ffn_p, ffn_s, v_s = [], [], [], [], [], [], []
    for l in range(DEPTH):
        w = (g_mix_pre[l], g_mix_post[l], g_ffn_pre[l], g_ffn_post[l], w_in[l], pool_w[l], pool_scale[l],
             conv_w[l], sgu_norm[l], sgu_w[l], sgu_b[l], w_out[l], ffn_up[l], ffn_conv_w[l],
             ffn_conv_b[l], ffn_down[l])
        hp, np_, nc_, nf_, _ = block(hp, zero_pool, zero_conv, zero_ffn, 0, *w)
        hs, ns_, ncs_, nfs_, vs_ = block(hs, state_pool[l], state_conv[l], state_ffn[l], PAST_LEN, *w)
        pool_p.append(np_)
        conv_p.append(nc_)
        ffn_p.append(nf_)
        pool_s.append(ns_)
        conv_s.append(ncs_)
        ffn_s.append(nfs_)
        v_s.append(vs_)
    return (hp, hs, jnp.stack(pool_p), jnp.stack(pool_s), jnp.stack(conv_p), jnp.stack(conv_s),
            jnp.stack(ffn_p), jnp.stack(ffn_s), jnp.stack(v_s))
```

```python
import functools

import jax
import jax.numpy as jnp
from jax import lax
from jax.experimental import pallas as pl
from jax.experimental.pallas import tpu as pltpu

F32 = jnp.float32
BF16 = jnp.bfloat16

EPS = 1e-6
PAST_LEN = 16384
POOL_WINDOWS = (2, 4, 8, 16)
POOL_HIST = max(POOL_WINDOWS) - 1
HEAD = 128
SHORT_CONV = 3
FFN_CONV = 3

TM = 512
TF = 512
VMEM_LIMIT = 56 * 1024 * 1024


def _rmsnorm(x, g):
    ms = jnp.mean(x * x, axis=-1, keepdims=True)
    return x * lax.rsqrt(ms + EPS) * g


def _gelu_tanh(x):
    c = 0.7978845608028654
    return x * (0.5 * (1.0 + jnp.tanh(c * (x + 0.044715 * (x * x * x)))))


def _params(*sem):
    return pltpu.CompilerParams(dimension_semantics=sem, vmem_limit_bytes=VMEM_LIMIT)


def _inproj_kernel(x_ref, g_ref, w_ref, o_ref, h_ref):
    @pl.when(pl.program_id(1) == 0)
    def _():
        h_ref[...] = _rmsnorm(x_ref[...], g_ref[...]).astype(BF16)

    o_ref[...] = jnp.dot(h_ref[...], w_ref[...], preferred_element_type=F32)


def _inproj(x, g, w):
    m, d = x.shape
    n = w.shape[1]
    tn = n // 2
    return pl.pallas_call(
        _inproj_kernel,
        out_shape=jax.ShapeDtypeStruct((m, n), F32),
        grid=(m // TM, n // tn),
        in_specs=[
            pl.BlockSpec((TM, d), lambda i, j: (i, 0)),
            pl.BlockSpec((1, d), lambda i, j: (0, 0)),
            pl.BlockSpec((d, tn), lambda i, j: (0, j)),
        ],
        out_specs=pl.BlockSpec((TM, tn), lambda i, j: (i, j)),
        scratch_shapes=[pltpu.VMEM((TM, d), BF16)],
        compiler_params=_params("parallel", "arbitrary"),
        name="inproj",
    )(x, g, w)


def _mix_prompt_kernel(tiles_per_seq, widths, p_ref, pw_ref, ps_ref, cw_ref, sn_ref, sw_ref,
                       sb_ref, o_ref, zt_ref, zzp_ref, zzc_ref):
    pool_w, conv_w, sgu_w = widths
    o1 = pool_w
    o2 = o1 + conv_w
    o3 = o2 + conv_w
    o4 = o3 + conv_w
    o5 = o4 + sgu_w
    ph = 16
    ch = 8
    tile_in_seq = pl.program_id(0) % tiles_per_seq

    @pl.when(tile_in_seq == 0)
    def _():
        zzp_ref[0:ph, :] = jnp.zeros((ph, pool_w), F32)
        zzc_ref[0:ch, :] = jnp.zeros((ch, conv_w), F32)

    @pl.when(tile_in_seq != 0)
    def _():
        zzp_ref[0:ph, :] = zzp_ref[TM:TM + ph, :]
        zzc_ref[0:ch, :] = zzc_ref[TM:TM + ch, :]

    zzp_ref[ph:ph + TM, :] = p_ref[:, 0:o1]
    pos = tile_in_seq * TM + lax.broadcasted_iota(jnp.int32, (TM, 1), 0)
    for g, w in enumerate(POOL_WINDOWS):
        c0, c1 = g * HEAD, (g + 1) * HEAD
        xa = p_ref[:, c0:c1]
        s = xa
        for j in range(1, w):
            s = s + zzp_ref[ph - j:ph - j + TM, c0:c1]
        cnt = jnp.minimum(w, pos + 1).astype(F32)
        pooled = s / cnt - xa
        y = jnp.dot(pooled.astype(BF16), pw_ref[g], preferred_element_type=F32)
        o_ref[:, c0:c1] = (y * ps_ref[:, c0:c1]).astype(BF16)

    z = p_ref[:, o3:o4] * p_ref[:, o1:o2]
    zzc_ref[ch:ch + TM, :] = z
    y = (cw_ref[2:3, :] * z + cw_ref[1:2, :] * zzc_ref[ch - 1:ch - 1 + TM, :]
         + cw_ref[0:1, :] * zzc_ref[ch - 2:ch - 2 + TM, :])
    o_ref[:, o1:o2] = (p_ref[:, o2:o3] * y).astype(BF16)
    zt_ref[...] = zzc_ref[TM:TM + ch, :]

    vn = _rmsnorm(p_ref[:, o5:o5 + sgu_w], sn_ref[...]).astype(BF16)
    row = lax.broadcasted_iota(jnp.int32, (HEAD, HEAD), 0)
    col = lax.broadcasted_iota(jnp.int32, (HEAD, HEAD), 1)
    for h in range(sgu_w // HEAD):
        wh = jnp.where(row >= col, sw_ref[h], jnp.zeros((HEAD, HEAD), BF16))
        h0, h1 = h * HEAD, (h + 1) * HEAD
        for c in range(TM // HEAD):
            r0, r1 = c * HEAD, (c + 1) * HEAD
            gate = jnp.dot(wh, vn[r0:r1, h0:h1], preferred_element_type=F32) + sb_ref[:, h0:h1]
            o_ref[r0:r1, o2 + h0:o2 + h1] = (p_ref[r0:r1, o4 + h0:o4 + h1] * gate).astype(BF16)


def _mix_prompt(p, n_rows, seq, total_rows, widths, pool_w, pool_scale, conv_w, sgu_norm, sgu_w, sgu_bx):
    pw, cw, sw = widths
    n_tiles = n_rows // TM
    full = lambda a: pl.BlockSpec(a.shape, lambda i: (0,) * a.ndim)
    return pl.pallas_call(
        functools.partial(_mix_prompt_kernel, seq // TM, widths),
        out_shape=(jax.ShapeDtypeStruct((total_rows, pw + cw + sw), BF16),
                   jax.ShapeDtypeStruct((n_tiles * 8, cw), F32)),
        grid=(n_tiles,),
        in_specs=[pl.BlockSpec((TM, p.shape[1]), lambda i: (i, 0)),
                  full(pool_w), full(pool_scale), full(conv_w), full(sgu_norm), full(sgu_w), full(sgu_bx)],
        out_specs=(pl.BlockSpec((TM, pw + cw + sw), lambda i: (i, 0)),
                   pl.BlockSpec((8, cw), lambda i: (i, 0))),
        scratch_shapes=[pltpu.VMEM((16 + TM, pw), F32), pltpu.VMEM((8 + TM, cw), F32)],
        compiler_params=_params("arbitrary"),
        name="mix_prompt",
    )(p, pool_w, pool_scale, conv_w, sgu_norm, sgu_w, sgu_bx)


def _mix_sample_kernel(dec_seq, widths, p_ref, sp_ref, sc_ref, pw_ref, ps_ref, cw_ref, sn_ref,
                       wx_ref, sb_ref, mix_in_ref, o_ref, zs_ref, vn_ref):
    del mix_in_ref
    pool_w, conv_w, sgu_w = widths
    o1 = pool_w
    o2 = o1 + conv_w
    o3 = o2 + conv_w
    o4 = o3 + conv_w
    o5 = o4 + sgu_w
    sb = TM // dec_seq
    slab = lambda t: slice(t * sb, (t + 1) * sb)

    def zz(k, c0, c1):
        if k < POOL_HIST:
            return sp_ref[slab(k), c0:c1]
        return p_ref[slab(k - POOL_HIST), c0:c1]

    for g, w in enumerate(POOL_WINDOWS):
        c0, c1 = g * HEAD, (g + 1) * HEAD
        for t in range(dec_seq):
            s = zz(POOL_HIST + t, c0, c1)
            for j in range(1, w):
                s = s + zz(POOL_HIST + t - j, c0, c1)
            cnt = float(min(w, PAST_LEN + t + 1))
            pooled = s / cnt - p_ref[slab(t), c0:c1]
            y = jnp.dot(pooled.astype(BF16), pw_ref[g], preferred_element_type=F32)
            o_ref[slab(t), c0:c1] = (y * ps_ref[:, c0:c1]).astype(BF16)

    def zc(k):
        if k < SHORT_CONV - 1:
            return sc_ref[slab(k), :]
        t = k - (SHORT_CONV - 1)
        return p_ref[slab(t), o3:o4] * p_ref[slab(t), o1:o2]

    for t in range(dec_seq):
        y = cw_ref[0:1, :] * zc(t) + cw_ref[1:2, :] * zc(t + 1) + cw_ref[2:3, :] * zc(t + 2)
        o_ref[slab(t), o1:o2] = (p_ref[slab(t), o2:o3] * y).astype(BF16)
    for k in range(SHORT_CONV - 1):
        zs_ref[slab(k), :] = zc(dec_seq + k)

    vn_ref[...] = _rmsnorm(p_ref[:, o5:o5 + sgu_w], sn_ref[...])
    for t in range(dec_seq):
        gate = sb_ref[t:t + 1, :] + wx_ref[t * dec_seq:t * dec_seq + 1, :] * vn_ref[slab(0), :]
        for s in range(1, t + 1):
            gate = gate + wx_ref[t * dec_seq + s:t * dec_seq + s + 1, :] * vn_ref[slab(s), :]
        o_ref[slab(t), o2:o2 + sgu_w] = (p_ref[slab(t), o4:o5] * gate).astype(BF16)


def _mix_sample(p, mix, row0, n_rows, dec_seq, widths, st_pool, st_conv, pool_w, pool_scale, conv_w,
                sgu_norm, sgu_wx, sgu_bx):
    pw, cw, sw = widths
    n_tiles = n_rows // TM
    t0 = row0 // TM
    sb = TM // dec_seq
    full = lambda a: pl.BlockSpec(a.shape, lambda i: (0,) * a.ndim)
    return pl.pallas_call(
        functools.partial(_mix_sample_kernel, dec_seq, widths),
        out_shape=(jax.ShapeDtypeStruct(mix.shape, BF16),
                   jax.ShapeDtypeStruct((n_tiles * (SHORT_CONV - 1) * sb, cw), F32),
                   jax.ShapeDtypeStruct((n_rows, sw), F32)),
        grid=(n_tiles,),
        in_specs=[pl.BlockSpec((TM, p.shape[1]), lambda i: (t0 + i, 0)),
                  pl.BlockSpec((POOL_HIST * sb, pw), lambda i: (i, 0)),
                  pl.BlockSpec(((SHORT_CONV - 1) * sb, cw), lambda i: (i, 0)),
                  full(pool_w), full(pool_scale), full(conv_w), full(sgu_norm), full(sgu_wx), full(sgu_bx),
                  pl.BlockSpec(memory_space=pl.ANY)],
        out_specs=(pl.BlockSpec((TM, pw + cw + sw), lambda i: (t0 + i, 0)),
                   pl.BlockSpec(((SHORT_CONV - 1) * sb, cw), lambda i: (i, 0)),
                   pl.BlockSpec((TM, sw), lambda i: (i, 0))),
        input_output_aliases={9: 0},
        compiler_params=_params("parallel"),
        name="mix_sample",
    )(p, st_pool, st_conv, pool_w, pool_scale, conv_w, sgu_norm, sgu_wx, sgu_bx, mix)


def _outproj_kernel(mix_ref, w_ref, x_ref, g_ref, o_ref):
    a = jnp.dot(mix_ref[...], w_ref[...], preferred_element_type=F32)
    o_ref[...] = x_ref[...] + _rmsnorm(a, g_ref[...])


def _outproj(mix, w, x, g):
    m, d = x.shape
    k = mix.shape[1]
    return pl.pallas_call(
        _outproj_kernel,
        out_shape=jax.ShapeDtypeStruct((m, d), F32),
        grid=(m // TM,),
        in_specs=[pl.BlockSpec((TM, k), lambda i: (i, 0)),
                  pl.BlockSpec((k, d), lambda i: (0, 0)),
                  pl.BlockSpec((TM, d), lambda i: (i, 0)),
                  pl.BlockSpec((1, d), lambda i: (0, 0))],
        out_specs=pl.BlockSpec((TM, d), lambda i: (i, 0)),
        compiler_params=_params("parallel"),
        name="outproj",
    )(mix, w, x, g)


def _ffn_kernel(shift, tiles_per_seq, has_prev, *refs):
    if has_prev:
        (x_ref, gpre_ref, upg_ref, upv_ref, cw_ref, cb_ref, dn_ref, gpost_ref, prev_ref, _,
         o_ref, gt_ref, hn_ref, acc_ref, gbuf_ref) = refs
    else:
        (x_ref, gpre_ref, upg_ref, upv_ref, cw_ref, cb_ref, dn_ref, gpost_ref,
         o_ref, gt_ref, hn_ref, acc_ref, gbuf_ref, carry_ref) = refs
    f = pl.program_id(1)
    nf = pl.num_programs(1)
    hist = gbuf_ref.shape[0] - TM
    keep = gt_ref.shape[0]

    @pl.when(f == 0)
    def _():
        hn_ref[...] = _rmsnorm(x_ref[...], gpre_ref[...]).astype(BF16)
        acc_ref[...] = jnp.zeros(acc_ref.shape, F32)

    gate = jnp.dot(hn_ref[...], upg_ref[...], preferred_element_type=F32)
    val = jnp.dot(hn_ref[...], upv_ref[...], preferred_element_type=F32)

    if has_prev:
        gbuf_ref[0:hist, :] = prev_ref[...]
    else:
        tile_in_seq = pl.program_id(0) % tiles_per_seq

        @pl.when(tile_in_seq == 0)
        def _():
            gbuf_ref[0:hist, :] = jnp.zeros((hist, gate.shape[1]), F32)

        @pl.when(tile_in_seq != 0)
        def _():
            gbuf_ref[0:hist, :] = carry_ref[f]

    gbuf_ref[hist:hist + TM, :] = gate
    tail = gbuf_ref[hist + TM - keep:hist + TM, :]
    gt_ref[...] = tail
    if not has_prev:
        carry_ref[f] = tail

    gc = (cw_ref[2:3, :] * gate
          + cw_ref[1:2, :] * gbuf_ref[hist - shift:hist - shift + TM, :]
          + cw_ref[0:1, :] * gbuf_ref[hist - 2 * shift:hist - 2 * shift + TM, :]
          + cb_ref[...])
    h = (_gelu_tanh(gc) * val).astype(BF16)
    acc_ref[...] += jnp.dot(h, dn_ref[...], preferred_element_type=F32)

    @pl.when(f == nf - 1)
    def _():
        o_ref[...] = x_ref[...] + _rmsnorm(acc_ref[...], gpost_ref[...])


def _ffn(x, row0, n_rows, g_pre, up, conv_w, conv_b, down, g_post, *, seq=None, dec_seq=None,
         prev=None, out_alias=None):
    m, d = x.shape
    dff = down.shape[0]
    nf = dff // TF
    n_tiles = n_rows // TM
    t0 = row0 // TM
    sample = prev is not None
    if sample:
        shift = TM // dec_seq
        hist = keep = (FFN_CONV - 1) * shift
        tiles_per_seq = None
    else:
        shift = 1
        hist = keep = 8
        tiles_per_seq = seq // TM
    in_specs = [pl.BlockSpec((TM, d), lambda i, j: (t0 + i, 0)),
                pl.BlockSpec((1, d), lambda i, j: (0, 0)),
                pl.BlockSpec((d, TF), lambda i, j: (0, j)),
                pl.BlockSpec((d, TF), lambda i, j: (0, nf + j)),
                pl.BlockSpec((FFN_CONV, TF), lambda i, j: (0, j)),
                pl.BlockSpec((1, TF), lambda i, j: (0, j)),
                pl.BlockSpec((TF, d), lambda i, j: (j, 0)),
                pl.BlockSpec((1, d), lambda i, j: (0, 0))]
    args = [x, g_pre, up, up, conv_w, conv_b, down, g_post]
    scratch = [pltpu.VMEM((TM, d), BF16), pltpu.VMEM((TM, d), F32), pltpu.VMEM((hist + TM, TF), F32)]
    aliases = {}
    if sample:
        in_specs += [pl.BlockSpec((hist, TF), lambda i, j: (i, j)), pl.BlockSpec(memory_space=pl.ANY)]
        args += [prev, out_alias]
        aliases = {9: 0}
    else:
        scratch.append(pltpu.VMEM((nf, keep, TF), F32))
    return pl.pallas_call(
        functools.partial(_ffn_kernel, shift, tiles_per_seq, sample),
        out_shape=(jax.ShapeDtypeStruct((m, d), F32),
                   jax.ShapeDtypeStruct((n_tiles * keep, dff), F32)),
        grid=(n_tiles, nf),
        in_specs=in_specs,
        out_specs=(pl.BlockSpec((TM, d), lambda i, j: (t0 + i, 0)),
                   pl.BlockSpec((keep, TF), lambda i, j: (i, j))),
        scratch_shapes=scratch,
        input_output_aliases=aliases,
        compiler_params=_params("arbitrary", "arbitrary"),
        name="ffn_sample" if sample else "ffn_prompt",
    )(*args)


def _to_tiles(a, sb):
    b, t, c = a.shape
    return a.reshape(b // sb, sb, t, c).transpose(0, 2, 1, 3).reshape(b * t, c)


def _from_tiles(a, t, sb):
    c = a.shape[-1]
    n = a.shape[0] // (t * sb)
    return a.reshape(n, t, sb, c).transpose(0, 2, 1, 3).reshape(n * sb, t, c)


def kernel(x_prompt, x_sample, state_pool, state_conv, state_ffn, g_mix_pre, g_mix_post, g_ffn_pre,
           g_ffn_post, w_in, pool_w, pool_scale, conv_w, sgu_norm, sgu_w, sgu_b, w_out, ffn_up,
           ffn_conv_w, ffn_conv_b, ffn_down):
    batch, seq, d = x_prompt.shape
    dec_batch, dec_seq, _ = x_sample.shape
    depth = w_in.shape[0]
    pw = pool_scale.shape[1]
    cw = conv_w.shape[2]
    sw = sgu_norm.shape[1]
    widths = (pw, cw, sw)
    n_heads = sgu_w.shape[1]
    assert seq % TM == 0 and TM % dec_seq == 0 and TM % HEAD == 0
    assert PAST_LEN % HEAD == 0 and dec_seq <= HEAD
    sb = TM // dec_seq
    assert dec_batch % sb == 0
    rows_p = batch * seq
    rows_s = dec_batch * dec_seq
    rows = rows_p + rows_s

    x = jnp.concatenate([x_prompt.reshape(rows_p, d), _to_tiles(x_sample, sb)], axis=0)

    w_in_b = w_in.astype(BF16)
    w_out_b = w_out.astype(BF16)
    up_b = ffn_up.astype(BF16)
    down_b = ffn_down.astype(BF16)
    pool_w_b = pool_w.astype(BF16)
    sgu_w_b = sgu_w.astype(BF16)
    sgu_bx = jnp.repeat(jnp.swapaxes(sgu_b, 1, 2), HEAD, axis=2)
    sgu_wx = jnp.repeat(jnp.transpose(sgu_w[:, :, :dec_seq, :dec_seq], (0, 2, 3, 1)), HEAD, axis=3)
    sgu_wx = sgu_wx.reshape(depth, dec_seq * dec_seq, sw)

    outs = {k: [] for k in ("pool_p", "pool_s", "conv_p", "conv_s", "ffn_p", "ffn_s", "v_s")}
    for l in range(depth):
        row = lambda a: a[l][None, :]
        st_pool = _to_tiles(state_pool[l], sb)
        st_conv = _to_tiles(state_conv[l], sb)
        st_ffn = _to_tiles(state_ffn[l], sb)

        p = _inproj(x, row(g_mix_pre), w_in_b[l])
        mix, zt = _mix_prompt(p, rows_p, seq, rows, widths, pool_w_b[l], row(pool_scale), conv_w[l],
                              row(sgu_norm), sgu_w_b[l], sgu_bx[l])
        mix, zs, vn = _mix_sample(p, mix, rows_p, rows_s, dec_seq, widths, st_pool, st_conv, pool_w_b[l],
                                  row(pool_scale), conv_w[l], row(sgu_norm), sgu_wx[l], sgu_bx[l])
        x1 = _outproj(mix, w_out_b[l], x, row(g_mix_post))
        x2, gt_p = _ffn(x1, 0, rows_p, row(g_ffn_pre), up_b[l], ffn_conv_w[l], row(ffn_conv_b), down_b[l],
                        row(g_ffn_post), seq=seq)
        x, gt_s = _ffn(x1, rows_p, rows_s, row(g_ffn_pre), up_b[l], ffn_conv_w[l], row(ffn_conv_b),
                       down_b[l], row(g_ffn_post), dec_seq=dec_seq, prev=st_ffn, out_alias=x2)

        tps = seq // TM
        xa_s = _from_tiles(p[rows_p:, :pw], dec_seq, sb)
        outs["pool_p"].append(p[:rows_p, :pw].reshape(batch, seq, pw)[:, seq - POOL_HIST:])
        outs["pool_s"].append(jnp.concatenate([state_pool[l], xa_s], axis=1)[:, -POOL_HIST:])
        outs["conv_p"].append(zt.reshape(batch, tps, 8, cw)[:, -1, 8 - (SHORT_CONV - 1):])
        outs["conv_s"].append(_from_tiles(zs, SHORT_CONV - 1, sb))
        outs["ffn_p"].append(gt_p.reshape(batch, tps, 8, -1)[:, -1, 8 - (FFN_CONV - 1):])
        outs["ffn_s"].append(_from_tiles(gt_s, FFN_CONV - 1, sb))
        outs["v_s"].append(_from_tiles(vn, dec_seq, sb))

    y_prompt = x[:rows_p].reshape(batch, seq, d)
    y_sample = _from_tiles(x[rows_p:], dec_seq, sb)
    st = lambda k: jnp.stack(outs[k])
    return (y_prompt, y_sample, st("pool_p"), st("pool_s"), st("conv_p"), st("conv_s"),
            st("ffn_p"), st("ffn_s"), st("v_s"))
```

```python
import functools

import jax
import jax.numpy as jnp
from jax import lax
from jax.experimental import pallas as pl
from jax.experimental.pallas import tpu as pltpu

F32 = jnp.float32
BF16 = jnp.bfloat16

EPS = 1e-6
PAST_LEN = 16384
POOL_WINDOWS = (2, 4, 8, 16)
POOL_HIST = max(POOL_WINDOWS) - 1
HEAD = 128
SHORT_CONV = 3
FFN_CONV = 3

TM = 512
TF = 512
VMEM_LIMIT = 56 * 1024 * 1024


def _rmsnorm(x, g):
    ms = jnp.mean(x * x, axis=-1, keepdims=True)
    return x * lax.rsqrt(ms + EPS) * g


def _gelu_tanh(x):
    c = 0.7978845608028654
    return x * (0.5 * (1.0 + jnp.tanh(c * (x + 0.044715 * (x * x * x)))))


def _params(*sem):
    return pltpu.CompilerParams(dimension_semantics=sem, vmem_limit_bytes=VMEM_LIMIT)


def _layer_spec(a, l, n_grid):
    zeros = (0,) * (a.ndim - 1)
    if n_grid == 1:
        return pl.BlockSpec((None,) + a.shape[1:], lambda i: (l,) + zeros)
    return pl.BlockSpec((None,) + a.shape[1:], lambda i, j: (l,) + zeros)


def _inproj_kernel(x_ref, g_ref, w_ref, o_ref, h_ref):
    @pl.when(pl.program_id(1) == 0)
    def _():
        h_ref[...] = _rmsnorm(x_ref[...], g_ref[...]).astype(BF16)

    o_ref[...] = jnp.dot(h_ref[...], w_ref[...], preferred_element_type=F32)


def _inproj(l, x, g, w):
    m, d = x.shape
    n = w.shape[2]
    tn = n // 2
    return pl.pallas_call(
        _inproj_kernel,
        out_shape=jax.ShapeDtypeStruct((m, n), F32),
        grid=(m // TM, n // tn),
        in_specs=[
            pl.BlockSpec((TM, d), lambda i, j: (i, 0)),
            _layer_spec(g, l, 2),
            pl.BlockSpec((None, d, tn), lambda i, j: (l, 0, j)),
        ],
        out_specs=pl.BlockSpec((TM, tn), lambda i, j: (i, j)),
        scratch_shapes=[pltpu.VMEM((TM, d), BF16)],
        compiler_params=_params("parallel", "arbitrary"),
        name="inproj",
    )(x, g, w)


def _mix_prompt_kernel(tiles_per_seq, widths, p_ref, pw_ref, ps_ref, cw_ref, sn_ref, sw_ref,
                       sb_ref, o_ref, zt_ref, zzp_ref, zzc_ref):
    pool_w, conv_w, sgu_w = widths
    o1 = pool_w
    o2 = o1 + conv_w
    o3 = o2 + conv_w
    o4 = o3 + conv_w
    o5 = o4 + sgu_w
    ph = 16
    ch = 8
    tile_in_seq = pl.program_id(0) % tiles_per_seq

    @pl.when(tile_in_seq == 0)
    def _():
        zzp_ref[0:ph, :] = jnp.zeros((ph, pool_w), F32)
        zzc_ref[0:ch, :] = jnp.zeros((ch, conv_w), F32)

    @pl.when(tile_in_seq != 0)
    def _():
        zzp_ref[0:ph, :] = zzp_ref[TM:TM + ph, :]
        zzc_ref[0:ch, :] = zzc_ref[TM:TM + ch, :]

    zzp_ref[ph:ph + TM, :] = p_ref[:, 0:o1]
    pos = tile_in_seq * TM + lax.broadcasted_iota(jnp.int32, (TM, 1), 0)
    for g, w in enumerate(POOL_WINDOWS):
        c0, c1 = g * HEAD, (g + 1) * HEAD
        xa = p_ref[:, c0:c1]
        s = xa
        for j in range(1, w):
            s = s + zzp_ref[ph - j:ph - j + TM, c0:c1]
        cnt = jnp.minimum(w, pos + 1).astype(F32)
        pooled = s / cnt - xa
        y = jnp.dot(pooled.astype(BF16), pw_ref[g], preferred_element_type=F32)
        o_ref[:, c0:c1] = (y * ps_ref[:, c0:c1]).astype(BF16)

    z = p_ref[:, o3:o4] * p_ref[:, o1:o2]
    zzc_ref[ch:ch + TM, :] = z
    y = (cw_ref[2:3, :] * z + cw_ref[1:2, :] * zzc_ref[ch - 1:ch - 1 + TM, :]
         + cw_ref[0:1, :] * zzc_ref[ch - 2:ch - 2 + TM, :])
    o_ref[:, o1:o2] = (p_ref[:, o2:o3] * y).astype(BF16)
    zt_ref[...] = zzc_ref[TM:TM + ch, :]

    vn = _rmsnorm(p_ref[:, o5:o5 + sgu_w], sn_ref[...]).astype(BF16)
    row = lax.broadcasted_iota(jnp.int32, (HEAD, HEAD), 0)
    col = lax.broadcasted_iota(jnp.int32, (HEAD, HEAD), 1)
    for h in range(sgu_w // HEAD):
        wh = jnp.where(row >= col, sw_ref[h], jnp.zeros((HEAD, HEAD), BF16))
        h0, h1 = h * HEAD, (h + 1) * HEAD
        for c in range(TM // HEAD):
            r0, r1 = c * HEAD, (c + 1) * HEAD
            gate = jnp.dot(wh, vn[r0:r1, h0:h1], preferred_element_type=F32) + sb_ref[:, h0:h1]
            o_ref[r0:r1, o2 + h0:o2 + h1] = (p_ref[r0:r1, o4 + h0:o4 + h1] * gate).astype(BF16)


def _mix_prompt(l, p, seq, widths, pool_w, pool_scale, conv_w, sgu_norm, sgu_w, sgu_bx):
    pw, cw, sw = widths
    n_rows = p.shape[0]
    n_tiles = n_rows // TM
    params = (pool_w, pool_scale, conv_w, sgu_norm, sgu_w, sgu_bx)
    return pl.pallas_call(
        functools.partial(_mix_prompt_kernel, seq // TM, widths),
        out_shape=(jax.ShapeDtypeStruct((n_rows, pw + cw + sw), BF16),
                   jax.ShapeDtypeStruct((n_tiles * 8, cw), F32)),
        grid=(n_tiles,),
        in_specs=[pl.BlockSpec((TM, p.shape[1]), lambda i: (i, 0))] + [_layer_spec(a, l, 1) for a in params],
        out_specs=(pl.BlockSpec((TM, pw + cw + sw), lambda i: (i, 0)),
                   pl.BlockSpec((8, cw), lambda i: (i, 0))),
        scratch_shapes=[pltpu.VMEM((16 + TM, pw), F32), pltpu.VMEM((8 + TM, cw), F32)],
        compiler_params=_params("arbitrary"),
        name="mix_prompt",
    )(p, *params)


def _mix_sample_kernel(dec_seq, widths, p_ref, sp_ref, sc_ref, pw_ref, ps_ref, cw_ref, sn_ref,
                       wx_ref, sb_ref, o_ref, zs_ref, vn_ref):
    pool_w, conv_w, sgu_w = widths
    o1 = pool_w
    o2 = o1 + conv_w
    o3 = o2 + conv_w
    o4 = o3 + conv_w
    o5 = o4 + sgu_w
    sb = TM // dec_seq
    slab = lambda t: slice(t * sb, (t + 1) * sb)

    def zz(k, c0, c1):
        if k < POOL_HIST:
            return sp_ref[slab(k), c0:c1]
        return p_ref[slab(k - POOL_HIST), c0:c1]

    for g, w in enumerate(POOL_WINDOWS):
        c0, c1 = g * HEAD, (g + 1) * HEAD
        for t in range(dec_seq):
            s = zz(POOL_HIST + t, c0, c1)
            for j in range(1, w):
                s = s + zz(POOL_HIST + t - j, c0, c1)
            cnt = float(min(w, PAST_LEN + t + 1))
            pooled = s / cnt - p_ref[slab(t), c0:c1]
            y = jnp.dot(pooled.astype(BF16), pw_ref[g], preferred_element_type=F32)
            o_ref[slab(t), c0:c1] = (y * ps_ref[:, c0:c1]).astype(BF16)

    def zc(k):
        if k < SHORT_CONV - 1:
            return sc_ref[slab(k), :]
        t = k - (SHORT_CONV - 1)
        return p_ref[slab(t), o3:o4] * p_ref[slab(t), o1:o2]

    for t in range(dec_seq):
        y = cw_ref[0:1, :] * zc(t) + cw_ref[1:2, :] * zc(t + 1) + cw_ref[2:3, :] * zc(t + 2)
        o_ref[slab(t), o1:o2] = (p_ref[slab(t), o2:o3] * y).astype(BF16)
    for k in range(SHORT_CONV - 1):
        zs_ref[slab(k), :] = zc(dec_seq + k)

    vn_ref[...] = _rmsnorm(p_ref[:, o5:o5 + sgu_w], sn_ref[...])
    for t in range(dec_seq):
        gate = sb_ref[t:t + 1, :] + wx_ref[t * dec_seq:t * dec_seq + 1, :] * vn_ref[slab(0), :]
        for s in range(1, t + 1):
            gate = gate + wx_ref[t * dec_seq + s:t * dec_seq + s + 1, :] * vn_ref[slab(s), :]
        o_ref[slab(t), o2:o2 + sgu_w] = (p_ref[slab(t), o4:o5] * gate).astype(BF16)


def _mix_sample(l, p, dec_seq, widths, st_pool, st_conv, pool_w, pool_scale, conv_w, sgu_norm, sgu_wx,
                sgu_bx):
    pw, cw, sw = widths
    n_rows = p.shape[0]
    n_tiles = n_rows // TM
    sb = TM // dec_seq
    params = (pool_w, pool_scale, conv_w, sgu_norm, sgu_wx, sgu_bx)
    return pl.pallas_call(
        functools.partial(_mix_sample_kernel, dec_seq, widths),
        out_shape=(jax.ShapeDtypeStruct((n_rows, pw + cw + sw), BF16),
                   jax.ShapeDtypeStruct((n_tiles * (SHORT_CONV - 1) * sb, cw), F32),
                   jax.ShapeDtypeStruct((n_rows, sw), F32)),
        grid=(n_tiles,),
        in_specs=[pl.BlockSpec((TM, p.shape[1]), lambda i: (i, 0)),
                  pl.BlockSpec((None, POOL_HIST * sb, pw), lambda i: (l, i, 0)),
                  pl.BlockSpec((None, (SHORT_CONV - 1) * sb, cw), lambda i: (l, i, 0))]
        + [_layer_spec(a, l, 1) for a in params],
        out_specs=(pl.BlockSpec((TM, pw + cw + sw), lambda i: (i, 0)),
                   pl.BlockSpec(((SHORT_CONV - 1) * sb, cw), lambda i: (i, 0)),
                   pl.BlockSpec((TM, sw), lambda i: (i, 0))),
        compiler_params=_params("parallel"),
        name="mix_sample",
    )(p, st_pool, st_conv, *params)


def _outproj_kernel(mix_ref, w_ref, x_ref, g_ref, o_ref):
    a = jnp.dot(mix_ref[...], w_ref[...], preferred_element_type=F32)
    o_ref[...] = x_ref[...] + _rmsnorm(a, g_ref[...])


def _outproj(l, mix, w, x, g):
    m, d = x.shape
    k = mix.shape[1]
    return pl.pallas_call(
        _outproj_kernel,
        out_shape=jax.ShapeDtypeStruct((m, d), F32),
        grid=(m // TM,),
        in_specs=[pl.BlockSpec((TM, k), lambda i: (i, 0)),
                  _layer_spec(w, l, 1),
                  pl.BlockSpec((TM, d), lambda i: (i, 0)),
                  _layer_spec(g, l, 1)],
        out_specs=pl.BlockSpec((TM, d), lambda i: (i, 0)),
        compiler_params=_params("parallel"),
        name="outproj",
    )(mix, w, x, g)


def _ffn_kernel(shift, tiles_per_seq, has_prev, *refs):
    if has_prev:
        (x_ref, gpre_ref, upg_ref, upv_ref, cw_ref, cb_ref, dn_ref, gpost_ref, prev_ref,
         o_ref, gt_ref, hn_ref, acc_ref, gbuf_ref) = refs
    else:
        (x_ref, gpre_ref, upg_ref, upv_ref, cw_ref, cb_ref, dn_ref, gpost_ref,
         o_ref, gt_ref, hn_ref, acc_ref, gbuf_ref, carry_ref) = refs
    f = pl.program_id(1)
    nf = pl.num_programs(1)
    hist = gbuf_ref.shape[0] - TM
    keep = gt_ref.shape[0]

    @pl.when(f == 0)
    def _():
        hn_ref[...] = _rmsnorm(x_ref[...], gpre_ref[...]).astype(BF16)
        acc_ref[...] = jnp.zeros(acc_ref.shape, F32)

    gate = jnp.dot(hn_ref[...], upg_ref[...], preferred_element_type=F32)
    val = jnp.dot(hn_ref[...], upv_ref[...], preferred_element_type=F32)

    if has_prev:
        gbuf_ref[0:hist, :] = prev_ref[...]
    else:
        tile_in_seq = pl.program_id(0) % tiles_per_seq

        @pl.when(tile_in_seq == 0)
        def _():
            gbuf_ref[0:hist, :] = jnp.zeros((hist, gate.shape[1]), F32)

        @pl.when(tile_in_seq != 0)
        def _():
            gbuf_ref[0:hist, :] = carry_ref[f]

    gbuf_ref[hist:hist + TM, :] = gate
    tail = gbuf_ref[hist + TM - keep:hist + TM, :]
    gt_ref[...] = tail
    if not has_prev:
        carry_ref[f] = tail

    gc = (cw_ref[2:3, :] * gate
          + cw_ref[1:2, :] * gbuf_ref[hist - shift:hist - shift + TM, :]
          + cw_ref[0:1, :] * gbuf_ref[hist - 2 * shift:hist - 2 * shift + TM, :]
          + cb_ref[...])
    h = (_gelu_tanh(gc) * val).astype(BF16)
    acc_ref[...] += jnp.dot(h, dn_ref[...], preferred_element_type=F32)

    @pl.when(f == nf - 1)
    def _():
        o_ref[...] = x_ref[...] + _rmsnorm(acc_ref[...], gpost_ref[...])


def _ffn(l, x, g_pre, up, conv_w, conv_b, down, g_post, *, seq=None, dec_seq=None, prev=None):
    m, d = x.shape
    dff = down.shape[1]
    nf = dff // TF
    n_tiles = m // TM
    sample = prev is not None
    if sample:
        shift = TM // dec_seq
        hist = keep = (FFN_CONV - 1) * shift
        tiles_per_seq = None
    else:
        shift = 1
        hist = keep = 8
        tiles_per_seq = seq // TM
    in_specs = [pl.BlockSpec((TM, d), lambda i, j: (i, 0)),
                _layer_spec(g_pre, l, 2),
                pl.BlockSpec((None, d, TF), lambda i, j: (l, 0, j)),
                pl.BlockSpec((None, d, TF), lambda i, j: (l, 0, nf + j)),
                pl.BlockSpec((None, FFN_CONV, TF), lambda i, j: (l, 0, j)),
                pl.BlockSpec((None, 1, TF), lambda i, j: (l, 0, j)),
                pl.BlockSpec((None, TF, d), lambda i, j: (l, j, 0)),
                _layer_spec(g_post, l, 2)]
    args = [x, g_pre, up, up, conv_w, conv_b, down, g_post]
    scratch = [pltpu.VMEM((TM, d), BF16), pltpu.VMEM((TM, d), F32), pltpu.VMEM((hist + TM, TF), F32)]
    if sample:
        in_specs.append(pl.BlockSpec((None, hist, TF), lambda i, j: (l, i, j)))
        args.append(prev)
    else:
        scratch.append(pltpu.VMEM((nf, keep, TF), F32))
    return pl.pallas_call(
        functools.partial(_ffn_kernel, shift, tiles_per_seq, sample),
        out_shape=(jax.ShapeDtypeStruct((m, d), F32),
                   jax.ShapeDtypeStruct((n_tiles * keep, dff), F32)),
        grid=(n_tiles, nf),
        in_specs=in_specs,
        out_specs=(pl.BlockSpec((TM, d), lambda i, j: (i, 0)),
                   pl.BlockSpec((keep, TF), lambda i, j: (i, j))),
        scratch_shapes=scratch,
        compiler_params=_params("arbitrary", "arbitrary"),
        name="ffn_sample" if sample else "ffn_prompt",
    )(*args)


def _to_tiles(a, sb):
    *lead, b, t, c = a.shape
    n = len(lead)
    a = a.reshape(*lead, b // sb, sb, t, c)
    a = jnp.swapaxes(a, n + 1, n + 2)
    return a.reshape(*lead, b * t, c)


def _from_tiles(a, t, sb):
    c = a.shape[-1]
    n = a.shape[0] // (t * sb)
    return a.reshape(n, t, sb, c).transpose(0, 2, 1, 3).reshape(n * sb, t, c)


def kernel(x_prompt, x_sample, state_pool, state_conv, state_ffn, g_mix_pre, g_mix_post, g_ffn_pre,
           g_ffn_post, w_in, pool_w, pool_scale, conv_w, sgu_norm, sgu_w, sgu_b, w_out, ffn_up,
           ffn_conv_w, ffn_conv_b, ffn_down):
    batch, seq, d = x_prompt.shape
    dec_batch, dec_seq, _ = x_sample.shape
    depth = w_in.shape[0]
    pw = pool_scale.shape[1]
    cw = conv_w.shape[2]
    sw = sgu_norm.shape[1]
    widths = (pw, cw, sw)
    assert seq % TM == 0 and TM % dec_seq == 0 and TM % HEAD == 0
    assert PAST_LEN % HEAD == 0 and dec_seq <= HEAD
    sb = TM // dec_seq
    assert dec_batch % sb == 0
    tps = seq // TM

    xp = x_prompt.reshape(batch * seq, d)
    xs = _to_tiles(x_sample, sb)
    st_pool = _to_tiles(state_pool, sb)
    st_conv = _to_tiles(state_conv, sb)
    st_ffn = _to_tiles(state_ffn, sb)

    vec = lambda a: a[:, None, :]
    g_mix_pre, g_mix_post, g_ffn_pre, g_ffn_post = map(vec, (g_mix_pre, g_mix_post, g_ffn_pre, g_ffn_post))
    pool_scale, sgu_norm, ffn_conv_b = map(vec, (pool_scale, sgu_norm, ffn_conv_b))
    w_in_b = w_in.astype(BF16)
    w_out_b = w_out.astype(BF16)
    up_b = ffn_up.astype(BF16)
    down_b = ffn_down.astype(BF16)
    pool_w_b = pool_w.astype(BF16)
    sgu_w_b = sgu_w.astype(BF16)
    sgu_bx = jnp.repeat(jnp.swapaxes(sgu_b, 1, 2), HEAD, axis=2)
    sgu_wx = jnp.repeat(jnp.transpose(sgu_w[:, :, :dec_seq, :dec_seq], (0, 2, 3, 1)), HEAD, axis=3)
    sgu_wx = sgu_wx.reshape(depth, dec_seq * dec_seq, sw)

    outs = {k: [] for k in ("pool_p", "pool_s", "conv_p", "conv_s", "ffn_p", "ffn_s", "v_s")}
    for l in range(depth):
        pp = _inproj(l, xp, g_mix_pre, w_in_b)
        ps = _inproj(l, xs, g_mix_pre, w_in_b)
        mix_p, zt = _mix_prompt(l, pp, seq, widths, pool_w_b, pool_scale, conv_w, sgu_norm, sgu_w_b, sgu_bx)
        mix_s, zs, vn = _mix_sample(l, ps, dec_seq, widths, st_pool, st_conv, pool_w_b, pool_scale, conv_w,
                                    sgu_norm, sgu_wx, sgu_bx)
        xp = _outproj(l, mix_p, w_out_b, xp, g_mix_post)
        xs = _outproj(l, mix_s, w_out_b, xs, g_mix_post)
        xp, gt_p = _ffn(l, xp, g_ffn_pre, up_b, ffn_conv_w, ffn_conv_b, down_b, g_ffn_post, seq=seq)
        xs, gt_s = _ffn(l, xs, g_ffn_pre, up_b, ffn_conv_w, ffn_conv_b, down_b, g_ffn_post,
                        dec_seq=dec_seq, prev=st_ffn)

        xa_s = _from_tiles(ps[:, :pw], dec_seq, sb)
        outs["pool_p"].append(pp.reshape(batch, seq, -1)[:, seq - POOL_HIST:, :pw])
        outs["pool_s"].append(jnp.concatenate([state_pool[l], xa_s], axis=1)[:, -POOL_HIST:])
        outs["conv_p"].append(zt.reshape(batch, tps, 8, cw)[:, -1, 8 - (SHORT_CONV - 1):])
        outs["conv_s"].append(_from_tiles(zs, SHORT_CONV - 1, sb))
        outs["ffn_p"].append(gt_p.reshape(batch, tps, 8, -1)[:, -1, 8 - (FFN_CONV - 1):])
        outs["ffn_s"].append(_from_tiles(gt_s, FFN_CONV - 1, sb))
        outs["v_s"].append(_from_tiles(vn, dec_seq, sb))

    st = lambda k: jnp.stack(outs[k])
    return (xp.reshape(batch, seq, d), _from_tiles(xs, dec_seq, sb), st("pool_p"), st("pool_s"),
            st("conv_p"), st("conv_s"), st("ffn_p"), st("ffn_s"), st("v_s"))
```

```python
import functools

import jax
import jax.numpy as jnp
from jax import lax
from jax.experimental import pallas as pl
from jax.experimental.pallas import tpu as pltpu

F32 = jnp.float32
BF16 = jnp.bfloat16

EPS = 1e-6
PAST_LEN = 16384
POOL_WINDOWS = (2, 4, 8, 16)
POOL_HIST = max(POOL_WINDOWS) - 1
HEAD = 128
SHORT_CONV = 3
FFN_CONV = 3

TM = 512
TF = 512
FFN_SPLIT = 2
VMEM_LIMIT = 56 * 1024 * 1024


def _rmsnorm(x, g):
    ms = jnp.mean(x * x, axis=-1, keepdims=True)
    return x * lax.rsqrt(ms + EPS) * g


def _gelu_tanh(x):
    c = 0.7978845608028654
    return x * (0.5 * (1.0 + jnp.tanh(c * (x + 0.044715 * (x * x * x)))))


def _params(*sem):
    return pltpu.CompilerParams(dimension_semantics=sem, vmem_limit_bytes=VMEM_LIMIT)


def _layer_spec(a, l, n_grid):
    zeros = (0,) * (a.ndim - 1)
    if n_grid == 1:
        return pl.BlockSpec((None,) + a.shape[1:], lambda i: (l,) + zeros)
    return pl.BlockSpec((None,) + a.shape[1:], lambda i, j: (l,) + zeros)


def _inproj_kernel(x_ref, g_ref, w_ref, o_ref, h_ref):
    @pl.when(pl.program_id(1) == 0)
    def _():
        h_ref[...] = _rmsnorm(x_ref[...], g_ref[...]).astype(BF16)

    o_ref[...] = jnp.dot(h_ref[...], w_ref[...], preferred_element_type=F32)


def _inproj(l, x, g, w):
    m, d = x.shape
    n = w.shape[2]
    tn = n // 2
    return pl.pallas_call(
        _inproj_kernel,
        out_shape=jax.ShapeDtypeStruct((m, n), F32),
        grid=(m // TM, n // tn),
        in_specs=[
            pl.BlockSpec((TM, d), lambda i, j: (i, 0)),
            _layer_spec(g, l, 2),
            pl.BlockSpec((None, d, tn), lambda i, j: (l, 0, j)),
        ],
        out_specs=pl.BlockSpec((TM, tn), lambda i, j: (i, j)),
        scratch_shapes=[pltpu.VMEM((TM, d), BF16)],
        compiler_params=_params("parallel", "arbitrary"),
        name="inproj",
    )(x, g, w)


def _mix_prompt_kernel(tiles_per_seq, widths, p_ref, pw_ref, ps_ref, cw_ref, sn_ref, sw_ref,
                       sb_ref, o_ref, zt_ref, zzp_ref, zzc_ref):
    pool_w, conv_w, sgu_w = widths
    o1 = pool_w
    o2 = o1 + conv_w
    o3 = o2 + conv_w
    o4 = o3 + conv_w
    o5 = o4 + sgu_w
    ph = 16
    ch = 8
    tile_in_seq = pl.program_id(0) % tiles_per_seq

    @pl.when(tile_in_seq == 0)
    def _():
        zzp_ref[0:ph, :] = jnp.zeros((ph, pool_w), F32)
        zzc_ref[0:ch, :] = jnp.zeros((ch, conv_w), F32)

    @pl.when(tile_in_seq != 0)
    def _():
        zzp_ref[0:ph, :] = zzp_ref[TM:TM + ph, :]
        zzc_ref[0:ch, :] = zzc_ref[TM:TM + ch, :]

    zzp_ref[ph:ph + TM, :] = p_ref[:, 0:o1]
    pos = tile_in_seq * TM + lax.broadcasted_iota(jnp.int32, (TM, 1), 0)
    for g, w in enumerate(POOL_WINDOWS):
        c0, c1 = g * HEAD, (g + 1) * HEAD
        xa = p_ref[:, c0:c1]
        s = xa
        for j in range(1, w):
            s = s + zzp_ref[ph - j:ph - j + TM, c0:c1]
        cnt = jnp.minimum(w, pos + 1).astype(F32)
        pooled = s / cnt - xa
        y = jnp.dot(pooled.astype(BF16), pw_ref[g], preferred_element_type=F32)
        o_ref[:, c0:c1] = (y * ps_ref[:, c0:c1]).astype(BF16)

    z = p_ref[:, o3:o4] * p_ref[:, o1:o2]
    zzc_ref[ch:ch + TM, :] = z
    y = (cw_ref[2:3, :] * z + cw_ref[1:2, :] * zzc_ref[ch - 1:ch - 1 + TM, :]
         + cw_ref[0:1, :] * zzc_ref[ch - 2:ch - 2 + TM, :])
    o_ref[:, o1:o2] = (p_ref[:, o2:o3] * y).astype(BF16)
    zt_ref[...] = zzc_ref[TM:TM + ch, :]

    vn = _rmsnorm(p_ref[:, o5:o5 + sgu_w], sn_ref[...]).astype(BF16)
    row = lax.broadcasted_iota(jnp.int32, (HEAD, HEAD), 0)
    col = lax.broadcasted_iota(jnp.int32, (HEAD, HEAD), 1)
    for h in range(sgu_w // HEAD):
        wh = jnp.where(row >= col, sw_ref[h], jnp.zeros((HEAD, HEAD), BF16))
        h0, h1 = h * HEAD, (h + 1) * HEAD
        for c in range(TM // HEAD):
            r0, r1 = c * HEAD, (c + 1) * HEAD
            gate = jnp.dot(wh, vn[r0:r1, h0:h1], preferred_element_type=F32) + sb_ref[:, h0:h1]
            o_ref[r0:r1, o2 + h0:o2 + h1] = (p_ref[r0:r1, o4 + h0:o4 + h1] * gate).astype(BF16)


def _mix_prompt(l, p, seq, widths, pool_w, pool_scale, conv_w, sgu_norm, sgu_w, sgu_bx):
    pw, cw, sw = widths
    n_rows = p.shape[0]
    n_tiles = n_rows // TM
    params = (pool_w, pool_scale, conv_w, sgu_norm, sgu_w, sgu_bx)
    return pl.pallas_call(
        functools.partial(_mix_prompt_kernel, seq // TM, widths),
        out_shape=(jax.ShapeDtypeStruct((n_rows, pw + cw + sw), BF16),
                   jax.ShapeDtypeStruct((n_tiles * 8, cw), F32)),
        grid=(n_tiles,),
        in_specs=[pl.BlockSpec((TM, p.shape[1]), lambda i: (i, 0))] + [_layer_spec(a, l, 1) for a in params],
        out_specs=(pl.BlockSpec((TM, pw + cw + sw), lambda i: (i, 0)),
                   pl.BlockSpec((8, cw), lambda i: (i, 0))),
        scratch_shapes=[pltpu.VMEM((16 + TM, pw), F32), pltpu.VMEM((8 + TM, cw), F32)],
        compiler_params=_params("arbitrary"),
        name="mix_prompt",
    )(p, *params)


def _mix_sample_kernel(dec_seq, widths, p_ref, sp_ref, sc_ref, pw_ref, ps_ref, cw_ref, sn_ref,
                       wx_ref, sb_ref, o_ref, zs_ref, vn_ref):
    pool_w, conv_w, sgu_w = widths
    o1 = pool_w
    o2 = o1 + conv_w
    o3 = o2 + conv_w
    o4 = o3 + conv_w
    o5 = o4 + sgu_w
    sb = TM // dec_seq
    slab = lambda t: slice(t * sb, (t + 1) * sb)

    def zz(k, c0, c1):
        if k < POOL_HIST:
            return sp_ref[slab(k), c0:c1]
        return p_ref[slab(k - POOL_HIST), c0:c1]

    for g, w in enumerate(POOL_WINDOWS):
        c0, c1 = g * HEAD, (g + 1) * HEAD
        for t in range(dec_seq):
            s = zz(POOL_HIST + t, c0, c1)
            for j in range(1, w):
                s = s + zz(POOL_HIST + t - j, c0, c1)
            cnt = float(min(w, PAST_LEN + t + 1))
            pooled = s / cnt - p_ref[slab(t), c0:c1]
            y = jnp.dot(pooled.astype(BF16), pw_ref[g], preferred_element_type=F32)
            o_ref[slab(t), c0:c1] = (y * ps_ref[:, c0:c1]).astype(BF16)

    def zc(k):
        if k < SHORT_CONV - 1:
            return sc_ref[slab(k), :]
        t = k - (SHORT_CONV - 1)
        return p_ref[slab(t), o3:o4] * p_ref[slab(t), o1:o2]

    for t in range(dec_seq):
        y = cw_ref[0:1, :] * zc(t) + cw_ref[1:2, :] * zc(t + 1) + cw_ref[2:3, :] * zc(t + 2)
        o_ref[slab(t), o1:o2] = (p_ref[slab(t), o2:o3] * y).astype(BF16)
    for k in range(SHORT_CONV - 1):
        zs_ref[slab(k), :] = zc(dec_seq + k)

    vn_ref[...] = _rmsnorm(p_ref[:, o5:o5 + sgu_w], sn_ref[...])
    for t in range(dec_seq):
        gate = sb_ref[t:t + 1, :] + wx_ref[t * dec_seq:t * dec_seq + 1, :] * vn_ref[slab(0), :]
        for s in range(1, t + 1):
            gate = gate + wx_ref[t * dec_seq + s:t * dec_seq + s + 1, :] * vn_ref[slab(s), :]
        o_ref[slab(t), o2:o2 + sgu_w] = (p_ref[slab(t), o4:o5] * gate).astype(BF16)


def _mix_sample(l, p, dec_seq, widths, st_pool, st_conv, pool_w, pool_scale, conv_w, sgu_norm, sgu_wx,
                sgu_bx):
    pw, cw, sw = widths
    n_rows = p.shape[0]
    n_tiles = n_rows // TM
    sb = TM // dec_seq
    params = (pool_w, pool_scale, conv_w, sgu_norm, sgu_wx, sgu_bx)
    return pl.pallas_call(
        functools.partial(_mix_sample_kernel, dec_seq, widths),
        out_shape=(jax.ShapeDtypeStruct((n_rows, pw + cw + sw), BF16),
                   jax.ShapeDtypeStruct((n_tiles * (SHORT_CONV - 1) * sb, cw), F32),
                   jax.ShapeDtypeStruct((n_rows, sw), F32)),
        grid=(n_tiles,),
        in_specs=[pl.BlockSpec((TM, p.shape[1]), lambda i: (i, 0)),
                  pl.BlockSpec((None, POOL_HIST * sb, pw), lambda i: (l, i, 0)),
                  pl.BlockSpec((None, (SHORT_CONV - 1) * sb, cw), lambda i: (l, i, 0))]
        + [_layer_spec(a, l, 1) for a in params],
        out_specs=(pl.BlockSpec((TM, pw + cw + sw), lambda i: (i, 0)),
                   pl.BlockSpec(((SHORT_CONV - 1) * sb, cw), lambda i: (i, 0)),
                   pl.BlockSpec((TM, sw), lambda i: (i, 0))),
        compiler_params=_params("parallel"),
        name="mix_sample",
    )(p, st_pool, st_conv, *params)


def _outproj_kernel(mix_ref, w_ref, x_ref, g_ref, o_ref):
    a = jnp.dot(mix_ref[...], w_ref[...], preferred_element_type=F32)
    o_ref[...] = x_ref[...] + _rmsnorm(a, g_ref[...])


def _outproj(l, mix, w, x, g):
    m, d = x.shape
    k = mix.shape[1]
    return pl.pallas_call(
        _outproj_kernel,
        out_shape=jax.ShapeDtypeStruct((m, d), F32),
        grid=(m // TM,),
        in_specs=[pl.BlockSpec((TM, k), lambda i: (i, 0)),
                  _layer_spec(w, l, 1),
                  pl.BlockSpec((TM, d), lambda i: (i, 0)),
                  _layer_spec(g, l, 1)],
        out_specs=pl.BlockSpec((TM, d), lambda i: (i, 0)),
        compiler_params=_params("parallel"),
        name="outproj",
    )(mix, w, x, g)


def _ffn_kernel(shift, tiles_per_seq, has_prev, *refs):
    if has_prev:
        (x_ref, gpre_ref, upg_ref, upv_ref, cw_ref, cb_ref, dn_ref, gpost_ref, prev_ref,
         o_ref, gt_ref, hn_ref, acc_ref, gbuf_ref) = refs
    else:
        (x_ref, gpre_ref, upg_ref, upv_ref, cw_ref, cb_ref, dn_ref, gpost_ref,
         o_ref, gt_ref, hn_ref, acc_ref, gbuf_ref, carry_ref) = refs
    f = pl.program_id(1)
    nf = pl.num_programs(1)
    hist = gbuf_ref.shape[0] - TM
    keep = gt_ref.shape[0]

    @pl.when(f == 0)
    def _():
        hn_ref[...] = _rmsnorm(x_ref[...], gpre_ref[...]).astype(BF16)
        acc_ref[...] = jnp.zeros(acc_ref.shape, F32)

    if has_prev:
        gbuf_ref[0:hist, :] = prev_ref[...]
    else:
        tile_in_seq = pl.program_id(0) % tiles_per_seq

        @pl.when(tile_in_seq == 0)
        def _():
            gbuf_ref[0:hist, :] = jnp.zeros((hist, TF), F32)

        @pl.when(tile_in_seq != 0)
        def _():
            gbuf_ref[0:hist, :] = carry_ref[f]

    cols = TF // FFN_SPLIT
    total = None
    for c in range(FFN_SPLIT):
        cs = slice(c * cols, (c + 1) * cols)
        gate = jnp.dot(hn_ref[...], upg_ref[:, cs], preferred_element_type=F32)
        val = jnp.dot(hn_ref[...], upv_ref[:, cs], preferred_element_type=F32)
        gbuf_ref[hist:hist + TM, cs] = gate
        gc = (cw_ref[2:3, cs] * gate
              + cw_ref[1:2, cs] * gbuf_ref[hist - shift:hist - shift + TM, cs]
              + cw_ref[0:1, cs] * gbuf_ref[hist - 2 * shift:hist - 2 * shift + TM, cs]
              + cb_ref[:, cs])
        h = (_gelu_tanh(gc) * val).astype(BF16)
        part = jnp.dot(h, dn_ref[cs, :], preferred_element_type=F32)
        total = part if total is None else total + part
    acc_ref[...] += total

    tail = gbuf_ref[hist + TM - keep:hist + TM, :]
    gt_ref[...] = tail
    if not has_prev:
        carry_ref[f] = tail

    @pl.when(f == nf - 1)
    def _():
        o_ref[...] = x_ref[...] + _rmsnorm(acc_ref[...], gpost_ref[...])


def _ffn(l, x, g_pre, up, conv_w, conv_b, down, g_post, *, seq=None, dec_seq=None, prev=None):
    m, d = x.shape
    dff = down.shape[1]
    nf = dff // TF
    n_tiles = m // TM
    sample = prev is not None
    if sample:
        shift = TM // dec_seq
        hist = keep = (FFN_CONV - 1) * shift
        tiles_per_seq = None
    else:
        shift = 1
        hist = keep = 8
        tiles_per_seq = seq // TM
    in_specs = [pl.BlockSpec((TM, d), lambda i, j: (i, 0)),
                _layer_spec(g_pre, l, 2),
                pl.BlockSpec((None, d, TF), lambda i, j: (l, 0, j)),
                pl.BlockSpec((None, d, TF), lambda i, j: (l, 0, nf + j)),
                pl.BlockSpec((None, FFN_CONV, TF), lambda i, j: (l, 0, j)),
                pl.BlockSpec((None, 1, TF), lambda i, j: (l, 0, j)),
                pl.BlockSpec((None, TF, d), lambda i, j: (l, j, 0)),
                _layer_spec(g_post, l, 2)]
    args = [x, g_pre, up, up, conv_w, conv_b, down, g_post]
    scratch = [pltpu.VMEM((TM, d), BF16), pltpu.VMEM((TM, d), F32), pltpu.VMEM((hist + TM, TF), F32)]
    if sample:
        in_specs.append(pl.BlockSpec((None, hist, TF), lambda i, j: (l, i, j)))
        args.append(prev)
    else:
        scratch.append(pltpu.VMEM((nf, keep, TF), F32))
    return pl.pallas_call(
        functools.partial(_ffn_kernel, shift, tiles_per_seq, sample),
        out_shape=(jax.ShapeDtypeStruct((m, d), F32),
                   jax.ShapeDtypeStruct((n_tiles * keep, dff), F32)),
        grid=(n_tiles, nf),
        in_specs=in_specs,
        out_specs=(pl.BlockSpec((TM, d), lambda i, j: (i, 0)),
                   pl.BlockSpec((keep, TF), lambda i, j: (i, j))),
        scratch_shapes=scratch,
        compiler_params=_params("arbitrary", "arbitrary"),
        name="ffn_sample" if sample else "ffn_prompt",
    )(*args)


def _to_tiles(a, sb):
    *lead, b, t, c = a.shape
    n = len(lead)
    a = a.reshape(*lead, b // sb, sb, t, c)
    a = jnp.swapaxes(a, n + 1, n + 2)
    return a.reshape(*lead, b * t, c)


def _from_tiles(a, t, sb):
    c = a.shape[-1]
    n = a.shape[0] // (t * sb)
    return a.reshape(n, t, sb, c).transpose(0, 2, 1, 3).reshape(n * sb, t, c)


def kernel(x_prompt, x_sample, state_pool, state_conv, state_ffn, g_mix_pre, g_mix_post, g_ffn_pre,
           g_ffn_post, w_in, pool_w, pool_scale, conv_w, sgu_norm, sgu_w, sgu_b, w_out, ffn_up,
           ffn_conv_w, ffn_conv_b, ffn_down):
    batch, seq, d = x_prompt.shape
    dec_batch, dec_seq, _ = x_sample.shape
    depth = w_in.shape[0]
    pw = pool_scale.shape[1]
    cw = conv_w.shape[2]
    sw = sgu_norm.shape[1]
    widths = (pw, cw, sw)
    assert seq % TM == 0 and TM % dec_seq == 0 and TM % HEAD == 0
    assert PAST_LEN % HEAD == 0 and dec_seq <= HEAD
    sb = TM // dec_seq
    assert dec_batch % sb == 0
    tps = seq // TM

    xp = x_prompt.reshape(batch * seq, d)
    xs = _to_tiles(x_sample, sb)
    st_pool = _to_tiles(state_pool, sb)
    st_conv = _to_tiles(state_conv, sb)
    st_ffn = _to_tiles(state_ffn, sb)

    vec = lambda a: a[:, None, :]
    g_mix_pre, g_mix_post, g_ffn_pre, g_ffn_post = map(vec, (g_mix_pre, g_mix_post, g_ffn_pre, g_ffn_post))
    pool_scale, sgu_norm, ffn_conv_b = map(vec, (pool_scale, sgu_norm, ffn_conv_b))
    w_in_b = w_in.astype(BF16)
    w_out_b = w_out.astype(BF16)
    up_b = ffn_up.astype(BF16)
    down_b = ffn_down.astype(BF16)
    pool_w_b = pool_w.astype(BF16)
    sgu_w_b = sgu_w.astype(BF16)
    sgu_bx = jnp.repeat(jnp.swapaxes(sgu_b, 1, 2), HEAD, axis=2)
    sgu_wx = jnp.repeat(jnp.transpose(sgu_w[:, :, :dec_seq, :dec_seq], (0, 2, 3, 1)), HEAD, axis=3)
    sgu_wx = sgu_wx.reshape(depth, dec_seq * dec_seq, sw)

    outs = {k: [] for k in ("pool_p", "pool_s", "conv_p", "conv_s", "ffn_p", "ffn_s", "v_s")}
    for l in range(depth):
        pp = _inproj(l, xp, g_mix_pre, w_in_b)
        ps = _inproj(l, xs, g_mix_pre, w_in_b)
        mix_p, zt = _mix_prompt(l, pp, seq, widths, pool_w_b, pool_scale, conv_w, sgu_norm, sgu_w_b, sgu_bx)
        mix_s, zs, vn = _mix_sample(l, ps, dec_seq, widths, st_pool, st_conv, pool_w_b, pool_scale, conv_w,
                                    sgu_norm, sgu_wx, sgu_bx)
        xp = _outproj(l, mix_p, w_out_b, xp, g_mix_post)
        xs = _outproj(l, mix_s, w_out_b, xs, g_mix_post)
        xp, gt_p = _ffn(l, xp, g_ffn_pre, up_b, ffn_conv_w, ffn_conv_b, down_b, g_ffn_post, seq=seq)
        xs, gt_s = _ffn(l, xs, g_ffn_pre, up_b, ffn_conv_w, ffn_conv_b, down_b, g_ffn_post,
                        dec_seq=dec_seq, prev=st_ffn)

        xa_s = _from_tiles(ps[:, :pw], dec_seq, sb)
        outs["pool_p"].append(pp.reshape(batch, seq, -1)[:, seq - POOL_HIST:, :pw])
        outs["pool_s"].append(jnp.concatenate([state_pool[l], xa_s], axis=1)[:, -POOL_HIST:])
        outs["conv_p"].append(zt.reshape(batch, tps, 8, cw)[:, -1, 8 - (SHORT_CONV - 1):])
        outs["conv_s"].append(_from_tiles(zs, SHORT_CONV - 1, sb))
        outs["ffn_p"].append(gt_p.reshape(batch, tps, 8, -1)[:, -1, 8 - (FFN_CONV - 1):])
        outs["ffn_s"].append(_from_tiles(gt_s, FFN_CONV - 1, sb))
        outs["v_s"].append(_from_tiles(vn, dec_seq, sb))

    st = lambda k: jnp.stack(outs[k])
    return (xp.reshape(batch, seq, d), _from_tiles(xs, dec_seq, sb), st("pool_p"), st("pool_s"),
            st("conv_p"), st("conv_s"), st("ffn_p"), st("ffn_s"), st("v_s"))
```

```python
import functools

import jax
import jax.numpy as jnp
from jax import lax
from jax.experimental import pallas as pl
from jax.experimental.pallas import tpu as pltpu

F32 = jnp.float32
BF16 = jnp.bfloat16

EPS = 1e-6
PAST_LEN = 16384
POOL_WINDOWS = (2, 4, 8, 16)
POOL_HIST = max(POOL_WINDOWS) - 1
HEAD = 128
SHORT_CONV = 3
FFN_CONV = 3

TM = 512
TF = 512
FFN_SPLIT = 2
VMEM_LIMIT = 56 * 1024 * 1024


def _rmsnorm(x, g):
    ms = jnp.mean(x * x, axis=-1, keepdims=True)
    return x * lax.rsqrt(ms + EPS) * g


def _gelu_tanh(x):
    c = 0.7978845608028654
    return x * (0.5 * (1.0 + jnp.tanh(c * (x + 0.044715 * (x * x * x)))))


def _params(*sem):
    return pltpu.CompilerParams(dimension_semantics=sem, vmem_limit_bytes=VMEM_LIMIT)


def _layer_spec(a, l, n_grid):
    zeros = (0,) * (a.ndim - 1)
    if n_grid == 1:
        return pl.BlockSpec((None,) + a.shape[1:], lambda i: (l,) + zeros)
    return pl.BlockSpec((None,) + a.shape[1:], lambda i, j: (l,) + zeros)


def _inproj_kernel(x_ref, g_ref, w_ref, o_ref, h_ref):
    @pl.when(pl.program_id(1) == 0)
    def _():
        h_ref[...] = _rmsnorm(x_ref[...], g_ref[...]).astype(BF16)

    o_ref[...] = jnp.dot(h_ref[...], w_ref[...], preferred_element_type=F32)


def _inproj(l, x, g, w):
    m, d = x.shape
    n = w.shape[2]
    tn = n // 2
    return pl.pallas_call(
        _inproj_kernel,
        out_shape=jax.ShapeDtypeStruct((m, n), F32),
        grid=(m // TM, n // tn),
        in_specs=[
            pl.BlockSpec((TM, d), lambda i, j: (i, 0)),
            _layer_spec(g, l, 2),
            pl.BlockSpec((None, d, tn), lambda i, j: (l, 0, j)),
        ],
        out_specs=pl.BlockSpec((TM, tn), lambda i, j: (i, j)),
        scratch_shapes=[pltpu.VMEM((TM, d), BF16)],
        compiler_params=_params("parallel", "arbitrary"),
        name="inproj",
    )(x, g, w)


def _mix_prompt_kernel(tiles_per_seq, widths, p_ref, pw_ref, ps_ref, cw_ref, sn_ref, sw_ref,
                       sb_ref, o_ref, zt_ref, zzp_ref, zzc_ref):
    pool_w, conv_w, sgu_w = widths
    o1 = pool_w
    o2 = o1 + conv_w
    o3 = o2 + conv_w
    o4 = o3 + conv_w
    o5 = o4 + sgu_w
    ph = 16
    ch = 8
    tile_in_seq = pl.program_id(0) % tiles_per_seq

    @pl.when(tile_in_seq == 0)
    def _():
        zzp_ref[0:ph, :] = jnp.zeros((ph, pool_w), F32)
        zzc_ref[0:ch, :] = jnp.zeros((ch, conv_w), F32)

    @pl.when(tile_in_seq != 0)
    def _():
        zzp_ref[0:ph, :] = zzp_ref[TM:TM + ph, :]
        zzc_ref[0:ch, :] = zzc_ref[TM:TM + ch, :]

    zzp_ref[ph:ph + TM, :] = p_ref[:, 0:o1]
    pos = tile_in_seq * TM + lax.broadcasted_iota(jnp.int32, (TM, 1), 0)
    for g, w in enumerate(POOL_WINDOWS):
        c0, c1 = g * HEAD, (g + 1) * HEAD
        xa = p_ref[:, c0:c1]
        s = xa
        for j in range(1, w):
            s = s + zzp_ref[ph - j:ph - j + TM, c0:c1]
        cnt = jnp.minimum(w, pos + 1).astype(F32)
        pooled = s / cnt - xa
        y = jnp.dot(pooled.astype(BF16), pw_ref[g], preferred_element_type=F32)
        o_ref[:, c0:c1] = (y * ps_ref[:, c0:c1]).astype(BF16)

    z = p_ref[:, o3:o4] * p_ref[:, o1:o2]
    zzc_ref[ch:ch + TM, :] = z
    y = (cw_ref[2:3, :] * z + cw_ref[1:2, :] * zzc_ref[ch - 1:ch - 1 + TM, :]
         + cw_ref[0:1, :] * zzc_ref[ch - 2:ch - 2 + TM, :])
    o_ref[:, o1:o2] = (p_ref[:, o2:o3] * y).astype(BF16)
    zt_ref[...] = zzc_ref[TM:TM + ch, :]

    vn = _rmsnorm(p_ref[:, o5:o5 + sgu_w], sn_ref[...]).astype(BF16)
    row = lax.broadcasted_iota(jnp.int32, (HEAD, HEAD), 0)
    col = lax.broadcasted_iota(jnp.int32, (HEAD, HEAD), 1)
    for h in range(sgu_w // HEAD):
        wh = jnp.where(row >= col, sw_ref[h], jnp.zeros((HEAD, HEAD), BF16))
        h0, h1 = h * HEAD, (h + 1) * HEAD
        for c in range(TM // HEAD):
            r0, r1 = c * HEAD, (c + 1) * HEAD
            gate = jnp.dot(wh, vn[r0:r1, h0:h1], preferred_element_type=F32) + sb_ref[:, h0:h1]
            o_ref[r0:r1, o2 + h0:o2 + h1] = (p_ref[r0:r1, o4 + h0:o4 + h1] * gate).astype(BF16)


def _mix_prompt(l, p, seq, widths, pool_w, pool_scale, conv_w, sgu_norm, sgu_w, sgu_bx):
    pw, cw, sw = widths
    n_rows = p.shape[0]
    n_tiles = n_rows // TM
    params = (pool_w, pool_scale, conv_w, sgu_norm, sgu_w, sgu_bx)
    return pl.pallas_call(
        functools.partial(_mix_prompt_kernel, seq // TM, widths),
        out_shape=(jax.ShapeDtypeStruct((n_rows, pw + cw + sw), BF16),
                   jax.ShapeDtypeStruct((n_tiles * 8, cw), F32)),
        grid=(n_tiles,),
        in_specs=[pl.BlockSpec((TM, p.shape[1]), lambda i: (i, 0))] + [_layer_spec(a, l, 1) for a in params],
        out_specs=(pl.BlockSpec((TM, pw + cw + sw), lambda i: (i, 0)),
                   pl.BlockSpec((8, cw), lambda i: (i, 0))),
        scratch_shapes=[pltpu.VMEM((16 + TM, pw), F32), pltpu.VMEM((8 + TM, cw), F32)],
        compiler_params=_params("arbitrary"),
        name="mix_prompt",
    )(p, *params)


def _mix_sample_kernel(dec_seq, widths, p_ref, sp_ref, sc_ref, pw_ref, ps_ref, cw_ref, sn_ref,
                       wx_ref, sb_ref, o_ref, zs_ref, vn_ref):
    pool_w, conv_w, sgu_w = widths
    o1 = pool_w
    o2 = o1 + conv_w
    o3 = o2 + conv_w
    o4 = o3 + conv_w
    o5 = o4 + sgu_w
    sb = TM // dec_seq
    slab = lambda t: slice(t * sb, (t + 1) * sb)

    def zz(k, c0, c1):
        if k < POOL_HIST:
            return sp_ref[slab(k), c0:c1]
        return p_ref[slab(k - POOL_HIST), c0:c1]

    for g, w in enumerate(POOL_WINDOWS):
        c0, c1 = g * HEAD, (g + 1) * HEAD
        for t in range(dec_seq):
            s = zz(POOL_HIST + t, c0, c1)
            for j in range(1, w):
                s = s + zz(POOL_HIST + t - j, c0, c1)
            cnt = float(min(w, PAST_LEN + t + 1))
            pooled = s / cnt - p_ref[slab(t), c0:c1]
            y = jnp.dot(pooled.astype(BF16), pw_ref[g], preferred_element_type=F32)
            o_ref[slab(t), c0:c1] = (y * ps_ref[:, c0:c1]).astype(BF16)

    def zc(k):
        if k < SHORT_CONV - 1:
            return sc_ref[slab(k), :]
        t = k - (SHORT_CONV - 1)
        return p_ref[slab(t), o3:o4] * p_ref[slab(t), o1:o2]

    for t in range(dec_seq):
        y = cw_ref[0:1, :] * zc(t) + cw_ref[1:2, :] * zc(t + 1) + cw_ref[2:3, :] * zc(t + 2)
        o_ref[slab(t), o1:o2] = (p_ref[slab(t), o2:o3] * y).astype(BF16)
    for k in range(SHORT_CONV - 1):
        zs_ref[slab(k), :] = zc(dec_seq + k)

    vn_ref[...] = _rmsnorm(p_ref[:, o5:o5 + sgu_w], sn_ref[...])
    for t in range(dec_seq):
        gate = sb_ref[t:t + 1, :] + wx_ref[t * dec_seq:t * dec_seq + 1, :] * vn_ref[slab(0), :]
        for s in range(1, t + 1):
            gate = gate + wx_ref[t * dec_seq + s:t * dec_seq + s + 1, :] * vn_ref[slab(s), :]
        o_ref[slab(t), o2:o2 + sgu_w] = (p_ref[slab(t), o4:o5] * gate).astype(BF16)


def _mix_sample(l, p, dec_seq, widths, st_pool, st_conv, pool_w, pool_scale, conv_w, sgu_norm, sgu_wx,
                sgu_bx):
    pw, cw, sw = widths
    n_rows = p.shape[0]
    n_tiles = n_rows // TM
    sb = TM // dec_seq
    params = (pool_w, pool_scale, conv_w, sgu_norm, sgu_wx, sgu_bx)
    return pl.pallas_call(
        functools.partial(_mix_sample_kernel, dec_seq, widths),
        out_shape=(jax.ShapeDtypeStruct((n_rows, pw + cw + sw), BF16),
                   jax.ShapeDtypeStruct((n_tiles * (SHORT_CONV - 1) * sb, cw), F32),
                   jax.ShapeDtypeStruct((n_rows, sw), F32)),
        grid=(n_tiles,),
        in_specs=[pl.BlockSpec((TM, p.shape[1]), lambda i: (i, 0)),
                  pl.BlockSpec((None, POOL_HIST * sb, pw), lambda i: (l, i, 0)),
                  pl.BlockSpec((None, (SHORT_CONV - 1) * sb, cw), lambda i: (l, i, 0))]
        + [_layer_spec(a, l, 1) for a in params],
        out_specs=(pl.BlockSpec((TM, pw + cw + sw), lambda i: (i, 0)),
                   pl.BlockSpec(((SHORT_CONV - 1) * sb, cw), lambda i: (i, 0)),
                   pl.BlockSpec((TM, sw), lambda i: (i, 0))),
        compiler_params=_params("parallel"),
        name="mix_sample",
    )(p, st_pool, st_conv, *params)


def _outproj_kernel(mix_ref, w_ref, x_ref, g_ref, o_ref):
    a = jnp.dot(mix_ref[...], w_ref[...], preferred_element_type=F32)
    o_ref[...] = x_ref[...] + _rmsnorm(a, g_ref[...])


def _outproj(l, mix, w, x, g):
    m, d = x.shape
    k = mix.shape[1]
    return pl.pallas_call(
        _outproj_kernel,
        out_shape=jax.ShapeDtypeStruct((m, d), F32),
        grid=(m // TM,),
        in_specs=[pl.BlockSpec((TM, k), lambda i: (i, 0)),
                  _layer_spec(w, l, 1),
                  pl.BlockSpec((TM, d), lambda i: (i, 0)),
                  _layer_spec(g, l, 1)],
        out_specs=pl.BlockSpec((TM, d), lambda i: (i, 0)),
        compiler_params=_params("parallel"),
        name="outproj",
    )(mix, w, x, g)


def _ffn_kernel(shift, tiles_per_seq, has_prev, *refs):
    if has_prev:
        (x_ref, gpre_ref, upg_ref, upv_ref, cw_ref, cb_ref, dn_ref, gpost_ref, prev_ref,
         o_ref, gt_ref, hn_ref, acc_ref, gbuf_ref) = refs
    else:
        (x_ref, gpre_ref, upg_ref, upv_ref, cw_ref, cb_ref, dn_ref, gpost_ref,
         o_ref, gt_ref, hn_ref, acc_ref, gbuf_ref, carry_ref) = refs
    f = pl.program_id(1)
    nf = pl.num_programs(1)
    hist = gbuf_ref.shape[0] - TM
    keep = gt_ref.shape[0]

    @pl.when(f == 0)
    def _():
        hn_ref[...] = _rmsnorm(x_ref[...], gpre_ref[...]).astype(BF16)
        acc_ref[...] = jnp.zeros(acc_ref.shape, F32)

    if has_prev:
        gbuf_ref[0:hist, :] = prev_ref[...]
    else:
        tile_in_seq = pl.program_id(0) % tiles_per_seq

        @pl.when(tile_in_seq == 0)
        def _():
            gbuf_ref[0:hist, :] = jnp.zeros((hist, TF), F32)

        @pl.when(tile_in_seq != 0)
        def _():
            gbuf_ref[0:hist, :] = carry_ref[f]

    cols = TF // FFN_SPLIT
    total = None
    for c in range(FFN_SPLIT):
        cs = slice(c * cols, (c + 1) * cols)
        gate = jnp.dot(hn_ref[...], upg_ref[:, cs], preferred_element_type=F32)
        val = jnp.dot(hn_ref[...], upv_ref[:, cs], preferred_element_type=F32)
        gbuf_ref[hist:hist + TM, cs] = gate
        gc = (cw_ref[2:3, cs] * gate
              + cw_ref[1:2, cs] * gbuf_ref[hist - shift:hist - shift + TM, cs]
              + cw_ref[0:1, cs] * gbuf_ref[hist - 2 * shift:hist - 2 * shift + TM, cs]
              + cb_ref[:, cs])
        h = (_gelu_tanh(gc) * val).astype(BF16)
        part = jnp.dot(h, dn_ref[cs, :], preferred_element_type=F32)
        total = part if total is None else total + part
    acc_ref[...] += total

    tail = gbuf_ref[hist + TM - keep:hist + TM, :]
    gt_ref[...] = tail
    if not has_prev:
        carry_ref[f] = tail

    @pl.when(f == nf - 1)
    def _():
        o_ref[...] = x_ref[...] + _rmsnorm(acc_ref[...], gpost_ref[...])


def _ffn(l, x, g_pre, up, conv_w, conv_b, down, g_post, *, seq=None, dec_seq=None, prev=None):
    m, d = x.shape
    dff = down.shape[1]
    nf = dff // TF
    n_tiles = m // TM
    sample = prev is not None
    if sample:
        shift = TM // dec_seq
        hist = keep = (FFN_CONV - 1) * shift
        tiles_per_seq = None
    else:
        shift = 1
        hist = keep = 8
        tiles_per_seq = seq // TM
    in_specs = [pl.BlockSpec((TM, d), lambda i, j: (i, 0)),
                _layer_spec(g_pre, l, 2),
                pl.BlockSpec((None, None, d, TF), lambda i, j: (l, j, 0, 0)),
                pl.BlockSpec((None, None, d, TF), lambda i, j: (l, nf + j, 0, 0)),
                pl.BlockSpec((None, FFN_CONV, TF), lambda i, j: (l, 0, j)),
                pl.BlockSpec((None, 1, TF), lambda i, j: (l, 0, j)),
                pl.BlockSpec((None, TF, d), lambda i, j: (l, j, 0)),
                _layer_spec(g_post, l, 2)]
    args = [x, g_pre, up, up, conv_w, conv_b, down, g_post]
    scratch = [pltpu.VMEM((TM, d), BF16), pltpu.VMEM((TM, d), F32), pltpu.VMEM((hist + TM, TF), F32)]
    if sample:
        in_specs.append(pl.BlockSpec((None, hist, TF), lambda i, j: (l, i, j)))
        args.append(prev)
    else:
        scratch.append(pltpu.VMEM((nf, keep, TF), F32))
    return pl.pallas_call(
        functools.partial(_ffn_kernel, shift, tiles_per_seq, sample),
        out_shape=(jax.ShapeDtypeStruct((m, d), F32),
                   jax.ShapeDtypeStruct((n_tiles * keep, dff), F32)),
        grid=(n_tiles, nf),
        in_specs=in_specs,
        out_specs=(pl.BlockSpec((TM, d), lambda i, j: (i, 0)),
                   pl.BlockSpec((keep, TF), lambda i, j: (i, j))),
        scratch_shapes=scratch,
        compiler_params=_params("arbitrary", "arbitrary"),
        name="ffn_sample" if sample else "ffn_prompt",
    )(*args)


def _to_tiles(a, sb):
    *lead, b, t, c = a.shape
    n = len(lead)
    a = a.reshape(*lead, b // sb, sb, t, c)
    a = jnp.swapaxes(a, n + 1, n + 2)
    return a.reshape(*lead, b * t, c)


def _from_tiles(a, t, sb):
    c = a.shape[-1]
    n = a.shape[0] // (t * sb)
    return a.reshape(n, t, sb, c).transpose(0, 2, 1, 3).reshape(n * sb, t, c)


def kernel(x_prompt, x_sample, state_pool, state_conv, state_ffn, g_mix_pre, g_mix_post, g_ffn_pre,
           g_ffn_post, w_in, pool_w, pool_scale, conv_w, sgu_norm, sgu_w, sgu_b, w_out, ffn_up,
           ffn_conv_w, ffn_conv_b, ffn_down):
    batch, seq, d = x_prompt.shape
    dec_batch, dec_seq, _ = x_sample.shape
    depth = w_in.shape[0]
    pw = pool_scale.shape[1]
    cw = conv_w.shape[2]
    sw = sgu_norm.shape[1]
    widths = (pw, cw, sw)
    assert seq % TM == 0 and TM % dec_seq == 0 and TM % HEAD == 0
    assert PAST_LEN % HEAD == 0 and dec_seq <= HEAD
    sb = TM // dec_seq
    assert dec_batch % sb == 0
    tps = seq // TM

    xp = x_prompt.reshape(batch * seq, d)
    xs = _to_tiles(x_sample, sb)
    st_pool = _to_tiles(state_pool, sb)
    st_conv = _to_tiles(state_conv, sb)
    st_ffn = _to_tiles(state_ffn, sb)

    vec = lambda a: a[:, None, :]
    g_mix_pre, g_mix_post, g_ffn_pre, g_ffn_post = map(vec, (g_mix_pre, g_mix_post, g_ffn_pre, g_ffn_post))
    pool_scale, sgu_norm, ffn_conv_b = map(vec, (pool_scale, sgu_norm, ffn_conv_b))
    w_in_b = w_in.astype(BF16)
    w_out_b = w_out.astype(BF16)
    up_b = jnp.transpose(ffn_up.astype(BF16).reshape(depth, d, -1, TF), (0, 2, 1, 3))
    down_b = ffn_down.astype(BF16)
    pool_w_b = pool_w.astype(BF16)
    sgu_w_b = sgu_w.astype(BF16)
    sgu_bx = jnp.repeat(jnp.swapaxes(sgu_b, 1, 2), HEAD, axis=2)
    sgu_wx = jnp.repeat(jnp.transpose(sgu_w[:, :, :dec_seq, :dec_seq], (0, 2, 3, 1)), HEAD, axis=3)
    sgu_wx = sgu_wx.reshape(depth, dec_seq * dec_seq, sw)

    outs = {k: [] for k in ("pool_p", "pool_s", "conv_p", "conv_s", "ffn_p", "ffn_s", "v_s")}
    for l in range(depth):
        pp = _inproj(l, xp, g_mix_pre, w_in_b)
        ps = _inproj(l, xs, g_mix_pre, w_in_b)
        mix_p, zt = _mix_prompt(l, pp, seq, widths, pool_w_b, pool_scale, conv_w, sgu_norm, sgu_w_b, sgu_bx)
        mix_s, zs, vn = _mix_sample(l, ps, dec_seq, widths, st_pool, st_conv, pool_w_b, pool_scale, conv_w,
                                    sgu_norm, sgu_wx, sgu_bx)
        xp = _outproj(l, mix_p, w_out_b, xp, g_mix_post)
        xs = _outproj(l, mix_s, w_out_b, xs, g_mix_post)
        xp, gt_p = _ffn(l, xp, g_ffn_pre, up_b, ffn_conv_w, ffn_conv_b, down_b, g_ffn_post, seq=seq)
        xs, gt_s = _ffn(l, xs, g_ffn_pre, up_b, ffn_conv_w, ffn_conv_b, down_b, g_ffn_post,
                        dec_seq=dec_seq, prev=st_ffn)

        xa_s = _from_tiles(ps[:, :pw], dec_seq, sb)
        outs["pool_p"].append(pp.reshape(batch, seq, -1)[:, seq - POOL_HIST:, :pw])
        outs["pool_s"].append(jnp.concatenate([state_pool[l], xa_s], axis=1)[:, -POOL_HIST:])
        outs["conv_p"].append(zt.reshape(batch, tps, 8, cw)[:, -1, 8 - (SHORT_CONV - 1):])
        outs["conv_s"].append(_from_tiles(zs, SHORT_CONV - 1, sb))
        outs["ffn_p"].append(gt_p.reshape(batch, tps, 8, -1)[:, -1, 8 - (FFN_CONV - 1):])
        outs["ffn_s"].append(_from_tiles(gt_s, FFN_CONV - 1, sb))
        outs["v_s"].append(_from_tiles(vn, dec_seq, sb))

    st = lambda k: jnp.stack(outs[k])
    return (xp.reshape(batch, seq, d), _from_tiles(xs, dec_seq, sb), st("pool_p"), st("pool_s"),
            st("conv_p"), st("conv_s"), st("ffn_p"), st("ffn_s"), st("v_s"))
```

```python
import functools

import jax
import jax.numpy as jnp
from jax import lax
from jax.experimental import pallas as pl
from jax.experimental.pallas import tpu as pltpu

F32 = jnp.float32
BF16 = jnp.bfloat16

EPS = 1e-6
PAST_LEN = 16384
POOL_WINDOWS = (2, 4, 8, 16)
POOL_HIST = max(POOL_WINDOWS) - 1
HEAD = 128
SHORT_CONV = 3
FFN_CONV = 3

TM = 512
TF = 512
FFN_SPLIT = 2
VMEM_LIMIT = 56 * 1024 * 1024


def _rmsnorm(x, g):
    ms = jnp.mean(x * x, axis=-1, keepdims=True)
    return x * lax.rsqrt(ms + EPS) * g


def _gelu_tanh(x):
    c = 0.7978845608028654
    return x * (0.5 * (1.0 + jnp.tanh(c * (x + 0.044715 * (x * x * x)))))


def _params(*sem):
    return pltpu.CompilerParams(dimension_semantics=sem, vmem_limit_bytes=VMEM_LIMIT)


def _layer_spec(a, l, n_grid):
    zeros = (0,) * (a.ndim - 1)
    if n_grid == 1:
        return pl.BlockSpec((None,) + a.shape[1:], lambda i: (l,) + zeros)
    return pl.BlockSpec((None,) + a.shape[1:], lambda i, j: (l,) + zeros)


def _inproj_kernel(x_ref, g_ref, w_ref, o_ref, h_ref):
    @pl.when(pl.program_id(1) == 0)
    def _():
        h_ref[...] = _rmsnorm(x_ref[...], g_ref[...]).astype(BF16)

    o_ref[...] = jnp.dot(h_ref[...], w_ref[...], preferred_element_type=F32)


def _inproj(l, x, g, w):
    m, d = x.shape
    n = w.shape[2]
    tn = n // 2
    return pl.pallas_call(
        _inproj_kernel,
        out_shape=jax.ShapeDtypeStruct((m, n), F32),
        grid=(m // TM, n // tn),
        in_specs=[
            pl.BlockSpec((TM, d), lambda i, j: (i, 0)),
            _layer_spec(g, l, 2),
            pl.BlockSpec((None, d, tn), lambda i, j: (l, 0, j)),
        ],
        out_specs=pl.BlockSpec((TM, tn), lambda i, j: (i, j)),
        scratch_shapes=[pltpu.VMEM((TM, d), BF16)],
        compiler_params=_params("parallel", "arbitrary"),
        name="inproj",
    )(x, g, w)


def _mix_prompt_kernel(tiles_per_seq, widths, p_ref, pw_ref, ps_ref, cw_ref, sn_ref, sw_ref,
                       sb_ref, o_ref, zt_ref, zzp_ref, zzc_ref):
    pool_w, conv_w, sgu_w = widths
    o1 = pool_w
    o2 = o1 + conv_w
    o3 = o2 + conv_w
    o4 = o3 + conv_w
    o5 = o4 + sgu_w
    ph = 16
    ch = 8
    tile_in_seq = pl.program_id(0) % tiles_per_seq

    @pl.when(tile_in_seq == 0)
    def _():
        zzp_ref[0:ph, :] = jnp.zeros((ph, pool_w), F32)
        zzc_ref[0:ch, :] = jnp.zeros((ch, conv_w), F32)

    @pl.when(tile_in_seq != 0)
    def _():
        zzp_ref[0:ph, :] = zzp_ref[TM:TM + ph, :]
        zzc_ref[0:ch, :] = zzc_ref[TM:TM + ch, :]

    zzp_ref[ph:ph + TM, :] = p_ref[:, 0:o1]
    pos = tile_in_seq * TM + lax.broadcasted_iota(jnp.int32, (TM, 1), 0)
    for g, w in enumerate(POOL_WINDOWS):
        c0, c1 = g * HEAD, (g + 1) * HEAD
        xa = p_ref[:, c0:c1]
        s = xa
        for j in range(1, w):
            s = s + zzp_ref[ph - j:ph - j + TM, c0:c1]
        cnt = jnp.minimum(w, pos + 1).astype(F32)
        pooled = s / cnt - xa
        y = jnp.dot(pooled.astype(BF16), pw_ref[g], preferred_element_type=F32)
        o_ref[:, c0:c1] = (y * ps_ref[:, c0:c1]).astype(BF16)

    z = p_ref[:, o3:o4] * p_ref[:, o1:o2]
    zzc_ref[ch:ch + TM, :] = z
    y = (cw_ref[2:3, :] * z + cw_ref[1:2, :] * zzc_ref[ch - 1:ch - 1 + TM, :]
         + cw_ref[0:1, :] * zzc_ref[ch - 2:ch - 2 + TM, :])
    o_ref[:, o1:o2] = (p_ref[:, o2:o3] * y).astype(BF16)
    zt_ref[...] = zzc_ref[TM:TM + ch, :]

    vn = _rmsnorm(p_ref[:, o5:o5 + sgu_w], sn_ref[...]).astype(BF16)
    row = lax.broadcasted_iota(jnp.int32, (HEAD, HEAD), 0)
    col = lax.broadcasted_iota(jnp.int32, (HEAD, HEAD), 1)
    for h in range(sgu_w // HEAD):
        wh = jnp.where(row >= col, sw_ref[h], jnp.zeros((HEAD, HEAD), BF16))
        h0, h1 = h * HEAD, (h + 1) * HEAD
        for c in range(TM // HEAD):
            r0, r1 = c * HEAD, (c + 1) * HEAD
            gate = jnp.dot(wh, vn[r0:r1, h0:h1], preferred_element_type=F32) + sb_ref[:, h0:h1]
            o_ref[r0:r1, o2 + h0:o2 + h1] = (p_ref[r0:r1, o4 + h0:o4 + h1] * gate).astype(BF16)


def _mix_prompt(l, p, seq, widths, pool_w, pool_scale, conv_w, sgu_norm, sgu_w, sgu_bx):
    pw, cw, sw = widths
    n_rows = p.shape[0]
    n_tiles = n_rows // TM
    params = (pool_w, pool_scale, conv_w, sgu_norm, sgu_w, sgu_bx)
    return pl.pallas_call(
        functools.partial(_mix_prompt_kernel, seq // TM, widths),
        out_shape=(jax.ShapeDtypeStruct((n_rows, pw + cw + sw), BF16),
                   jax.ShapeDtypeStruct((n_tiles * 8, cw), F32)),
        grid=(n_tiles,),
        in_specs=[pl.BlockSpec((TM, p.shape[1]), lambda i: (i, 0))] + [_layer_spec(a, l, 1) for a in params],
        out_specs=(pl.BlockSpec((TM, pw + cw + sw), lambda i: (i, 0)),
                   pl.BlockSpec((8, cw), lambda i: (i, 0))),
        scratch_shapes=[pltpu.VMEM((16 + TM, pw), F32), pltpu.VMEM((8 + TM, cw), F32)],
        compiler_params=_params("arbitrary"),
        name="mix_prompt",
    )(p, *params)


def _mix_sample_kernel(dec_seq, widths, p_ref, sp_ref, sc_ref, pw_ref, ps_ref, cw_ref, sn_ref,
                       wx_ref, sb_ref, o_ref, zs_ref, vn_ref):
    pool_w, conv_w, sgu_w = widths
    o1 = pool_w
    o2 = o1 + conv_w
    o3 = o2 + conv_w
    o4 = o3 + conv_w
    o5 = o4 + sgu_w
    sb = TM // dec_seq
    slab = lambda t: slice(t * sb, (t + 1) * sb)

    def zz(k, c0, c1):
        if k < POOL_HIST:
            return sp_ref[slab(k), c0:c1]
        return p_ref[slab(k - POOL_HIST), c0:c1]

    for g, w in enumerate(POOL_WINDOWS):
        c0, c1 = g * HEAD, (g + 1) * HEAD
        for t in range(dec_seq):
            s = zz(POOL_HIST + t, c0, c1)
            for j in range(1, w):
                s = s + zz(POOL_HIST + t - j, c0, c1)
            cnt = float(min(w, PAST_LEN + t + 1))
            pooled = s / cnt - p_ref[slab(t), c0:c1]
            y = jnp.dot(pooled.astype(BF16), pw_ref[g], preferred_element_type=F32)
            o_ref[slab(t), c0:c1] = (y * ps_ref[:, c0:c1]).astype(BF16)

    def zc(k):
        if k < SHORT_CONV - 1:
            return sc_ref[slab(k), :]
        t = k - (SHORT_CONV - 1)
        return p_ref[slab(t), o3:o4] * p_ref[slab(t), o1:o2]

    for t in range(dec_seq):
        y = cw_ref[0:1, :] * zc(t) + cw_ref[1:2, :] * zc(t + 1) + cw_ref[2:3, :] * zc(t + 2)
        o_ref[slab(t), o1:o2] = (p_ref[slab(t), o2:o3] * y).astype(BF16)
    for k in range(SHORT_CONV - 1):
        zs_ref[slab(k), :] = zc(dec_seq + k)

    vn_ref[...] = _rmsnorm(p_ref[:, o5:o5 + sgu_w], sn_ref[...])
    for t in range(dec_seq):
        gate = sb_ref[t:t + 1, :] + wx_ref[t * dec_seq:t * dec_seq + 1, :] * vn_ref[slab(0), :]
        for s in range(1, t + 1):
            gate = gate + wx_ref[t * dec_seq + s:t * dec_seq + s + 1, :] * vn_ref[slab(s), :]
        o_ref[slab(t), o2:o2 + sgu_w] = (p_ref[slab(t), o4:o5] * gate).astype(BF16)


def _mix_sample(l, p, dec_seq, widths, st_pool, st_conv, pool_w, pool_scale, conv_w, sgu_norm, sgu_wx,
                sgu_bx):
    pw, cw, sw = widths
    n_rows = p.shape[0]
    n_tiles = n_rows // TM
    sb = TM // dec_seq
    params = (pool_w, pool_scale, conv_w, sgu_norm, sgu_wx, sgu_bx)
    return pl.pallas_call(
        functools.partial(_mix_sample_kernel, dec_seq, widths),
        out_shape=(jax.ShapeDtypeStruct((n_rows, pw + cw + sw), BF16),
                   jax.ShapeDtypeStruct((n_tiles * (SHORT_CONV - 1) * sb, cw), F32),
                   jax.ShapeDtypeStruct((n_rows, sw), F32)),
        grid=(n_tiles,),
        in_specs=[pl.BlockSpec((TM, p.shape[1]), lambda i: (i, 0)),
                  pl.BlockSpec((None, POOL_HIST * sb, pw), lambda i: (l, i, 0)),
                  pl.BlockSpec((None, (SHORT_CONV - 1) * sb, cw), lambda i: (l, i, 0))]
        + [_layer_spec(a, l, 1) for a in params],
        out_specs=(pl.BlockSpec((TM, pw + cw + sw), lambda i: (i, 0)),
                   pl.BlockSpec(((SHORT_CONV - 1) * sb, cw), lambda i: (i, 0)),
                   pl.BlockSpec((TM, sw), lambda i: (i, 0))),
        compiler_params=_params("parallel"),
        name="mix_sample",
    )(p, st_pool, st_conv, *params)


def _outproj_kernel(mix_ref, w_ref, x_ref, g_ref, o_ref):
    a = jnp.dot(mix_ref[...], w_ref[...], preferred_element_type=F32)
    o_ref[...] = x_ref[...] + _rmsnorm(a, g_ref[...])


def _outproj(l, mix, w, x, g):
    m, d = x.shape
    k = mix.shape[1]
    return pl.pallas_call(
        _outproj_kernel,
        out_shape=jax.ShapeDtypeStruct((m, d), F32),
        grid=(m // TM,),
        in_specs=[pl.BlockSpec((TM, k), lambda i: (i, 0)),
                  _layer_spec(w, l, 1),
                  pl.BlockSpec((TM, d), lambda i: (i, 0)),
                  _layer_spec(g, l, 1)],
        out_specs=pl.BlockSpec((TM, d), lambda i: (i, 0)),
        compiler_params=_params("parallel"),
        name="outproj",
    )(mix, w, x, g)


def _ffn_kernel(shift, tiles_per_seq, has_prev, *refs):
    if has_prev:
        (x_ref, gpre_ref, upg_ref, upv_ref, cw_ref, cb_ref, dn_ref, gpost_ref, prev_ref,
         o_ref, gt_ref, hn_ref, acc_ref, gbuf_ref) = refs
    else:
        (x_ref, gpre_ref, upg_ref, upv_ref, cw_ref, cb_ref, dn_ref, gpost_ref,
         o_ref, gt_ref, hn_ref, acc_ref, gbuf_ref, carry_ref) = refs
    f = pl.program_id(1)
    nf = pl.num_programs(1)
    hist = gbuf_ref.shape[0] - TM
    keep = gt_ref.shape[0]

    @pl.when(f == 0)
    def _():
        hn_ref[...] = _rmsnorm(x_ref[...], gpre_ref[...]).astype(BF16)
        acc_ref[...] = jnp.zeros(acc_ref.shape, F32)

    if has_prev:
        gbuf_ref[0:hist, :] = prev_ref[...]
    else:
        tile_in_seq = pl.program_id(0) % tiles_per_seq

        @pl.when(tile_in_seq == 0)
        def _():
            gbuf_ref[0:hist, :] = jnp.zeros((hist, TF), F32)

        @pl.when(tile_in_seq != 0)
        def _():
            gbuf_ref[0:hist, :] = carry_ref[f]

    cols = TF // FFN_SPLIT
    total = None
    for c in range(FFN_SPLIT):
        cs = slice(c * cols, (c + 1) * cols)
        gate = jnp.dot(hn_ref[...], upg_ref[:, cs], preferred_element_type=F32)
        val = jnp.dot(hn_ref[...], upv_ref[:, cs], preferred_element_type=F32)
        gbuf_ref[hist:hist + TM, cs] = gate
        gc = (cw_ref[2:3, cs] * gate
              + cw_ref[1:2, cs] * gbuf_ref[hist - shift:hist - shift + TM, cs]
              + cw_ref[0:1, cs] * gbuf_ref[hist - 2 * shift:hist - 2 * shift + TM, cs]
              + cb_ref[:, cs])
        h = (_gelu_tanh(gc) * val).astype(BF16)
        part = jnp.dot(h, dn_ref[cs, :].astype(BF16), preferred_element_type=F32)
        total = part if total is None else total + part
    acc_ref[...] += total

    tail = gbuf_ref[hist + TM - keep:hist + TM, :]
    gt_ref[...] = tail
    if not has_prev:
        carry_ref[f] = tail

    @pl.when(f == nf - 1)
    def _():
        o_ref[...] = x_ref[...] + _rmsnorm(acc_ref[...], gpost_ref[...])


def _ffn(l, x, g_pre, up, conv_w, conv_b, down, g_post, *, seq=None, dec_seq=None, prev=None):
    m, d = x.shape
    dff = down.shape[1]
    nf = dff // TF
    n_tiles = m // TM
    sample = prev is not None
    if sample:
        shift = TM // dec_seq
        hist = keep = (FFN_CONV - 1) * shift
        tiles_per_seq = None
    else:
        shift = 1
        hist = keep = 8
        tiles_per_seq = seq // TM
    in_specs = [pl.BlockSpec((TM, d), lambda i, j: (i, 0)),
                _layer_spec(g_pre, l, 2),
                pl.BlockSpec((None, d, TF), lambda i, j: (l, 0, j)),
                pl.BlockSpec((None, d, TF), lambda i, j: (l, 0, nf + j)),
                pl.BlockSpec((None, FFN_CONV, TF), lambda i, j: (l, 0, j)),
                pl.BlockSpec((None, 1, TF), lambda i, j: (l, 0, j)),
                pl.BlockSpec((None, TF, d), lambda i, j: (l, j, 0)),
                _layer_spec(g_post, l, 2)]
    args = [x, g_pre, up, up, conv_w, conv_b, down, g_post]
    scratch = [pltpu.VMEM((TM, d), BF16), pltpu.VMEM((TM, d), F32), pltpu.VMEM((hist + TM, TF), F32)]
    if sample:
        in_specs.append(pl.BlockSpec((None, hist, TF), lambda i, j: (l, i, j)))
        args.append(prev)
    else:
        scratch.append(pltpu.VMEM((nf, keep, TF), F32))
    return pl.pallas_call(
        functools.partial(_ffn_kernel, shift, tiles_per_seq, sample),
        out_shape=(jax.ShapeDtypeStruct((m, d), F32),
                   jax.ShapeDtypeStruct((n_tiles * keep, dff), F32)),
        grid=(n_tiles, nf),
        in_specs=in_specs,
        out_specs=(pl.BlockSpec((TM, d), lambda i, j: (i, 0)),
                   pl.BlockSpec((keep, TF), lambda i, j: (i, j))),
        scratch_shapes=scratch,
        compiler_params=_params("arbitrary", "arbitrary"),
        name="ffn_sample" if sample else "ffn_prompt",
    )(*args)


def _to_tiles(a, sb):
    *lead, b, t, c = a.shape
    n = len(lead)
    a = a.reshape(*lead, b // sb, sb, t, c)
    a = jnp.swapaxes(a, n + 1, n + 2)
    return a.reshape(*lead, b * t, c)


def _from_tiles(a, t, sb):
    c = a.shape[-1]
    n = a.shape[0] // (t * sb)
    return a.reshape(n, t, sb, c).transpose(0, 2, 1, 3).reshape(n * sb, t, c)


def kernel(x_prompt, x_sample, state_pool, state_conv, state_ffn, g_mix_pre, g_mix_post, g_ffn_pre,
           g_ffn_post, w_in, pool_w, pool_scale, conv_w, sgu_norm, sgu_w, sgu_b, w_out, ffn_up,
           ffn_conv_w, ffn_conv_b, ffn_down):
    batch, seq, d = x_prompt.shape
    dec_batch, dec_seq, _ = x_sample.shape
    depth = w_in.shape[0]
    pw = pool_scale.shape[1]
    cw = conv_w.shape[2]
    sw = sgu_norm.shape[1]
    widths = (pw, cw, sw)
    assert seq % TM == 0 and TM % dec_seq == 0 and TM % HEAD == 0
    assert PAST_LEN % HEAD == 0 and dec_seq <= HEAD
    sb = TM // dec_seq
    assert dec_batch % sb == 0
    tps = seq // TM

    xp = x_prompt.reshape(batch * seq, d)
    xs = _to_tiles(x_sample, sb)
    st_pool = _to_tiles(state_pool, sb)
    st_conv = _to_tiles(state_conv, sb)
    st_ffn = _to_tiles(state_ffn, sb)

    vec = lambda a: a[:, None, :]
    g_mix_pre, g_mix_post, g_ffn_pre, g_ffn_post = map(vec, (g_mix_pre, g_mix_post, g_ffn_pre, g_ffn_post))
    pool_scale, sgu_norm, ffn_conv_b = map(vec, (pool_scale, sgu_norm, ffn_conv_b))
    w_in_b = w_in.astype(BF16)
    w_out_b = w_out.astype(BF16)
    up_b = ffn_up.astype(BF16)
    pool_w_b = pool_w.astype(BF16)
    sgu_w_b = sgu_w.astype(BF16)
    sgu_bx = jnp.repeat(jnp.swapaxes(sgu_b, 1, 2), HEAD, axis=2)
    sgu_wx = jnp.repeat(jnp.transpose(sgu_w[:, :, :dec_seq, :dec_seq], (0, 2, 3, 1)), HEAD, axis=3)
    sgu_wx = sgu_wx.reshape(depth, dec_seq * dec_seq, sw)

    outs = {k: [] for k in ("pool_p", "pool_s", "conv_p", "conv_s", "ffn_p", "ffn_s", "v_s")}
    for l in range(depth):
        pp = _inproj(l, xp, g_mix_pre, w_in_b)
        ps = _inproj(l, xs, g_mix_pre, w_in_b)
        mix_p, zt = _mix_prompt(l, pp, seq, widths, pool_w_b, pool_scale, conv_w, sgu_norm, sgu_w_b, sgu_bx)
        mix_s, zs, vn = _mix_sample(l, ps, dec_seq, widths, st_pool, st_conv, pool_w_b, pool_scale, conv_w,
                                    sgu_norm, sgu_wx, sgu_bx)
        xp = _outproj(l, mix_p, w_out_b, xp, g_mix_post)
        xs = _outproj(l, mix_s, w_out_b, xs, g_mix_post)
        xp, gt_p = _ffn(l, xp, g_ffn_pre, up_b, ffn_conv_w, ffn_conv_b, ffn_down, g_ffn_post, seq=seq)
        xs, gt_s = _ffn(l, xs, g_ffn_pre, up_b, ffn_conv_w, ffn_conv_b, ffn_down, g_ffn_post,
                        dec_seq=dec_seq, prev=st_ffn)

        xa_s = _from_tiles(ps[:, :pw], dec_seq, sb)
        outs["pool_p"].append(pp.reshape(batch, seq, -1)[:, seq - POOL_HIST:, :pw])
        outs["pool_s"].append(jnp.concatenate([state_pool[l], xa_s], axis=1)[:, -POOL_HIST:])
        outs["conv_p"].append(zt.reshape(batch, tps, 8, cw)[:, -1, 8 - (SHORT_CONV - 1):])
        outs["conv_s"].append(_from_tiles(zs, SHORT_CONV - 1, sb))
        outs["ffn_p"].append(gt_p.reshape(batch, tps, 8, -1)[:, -1, 8 - (FFN_CONV - 1):])
        outs["ffn_s"].append(_from_tiles(gt_s, FFN_CONV - 1, sb))
        outs["v_s"].append(_from_tiles(vn, dec_seq, sb))

    st = lambda k: jnp.stack(outs[k])
    return (xp.reshape(batch, seq, d), _from_tiles(xs, dec_seq, sb), st("pool_p"), st("pool_s"),
            st("conv_p"), st("conv_s"), st("ffn_p"), st("ffn_s"), st("v_s"))
```

```python
import functools

import jax
import jax.numpy as jnp
from jax import lax
from jax.experimental import pallas as pl
from jax.experimental.pallas import tpu as pltpu

F32 = jnp.float32
BF16 = jnp.bfloat16

EPS = 1e-6
PAST_LEN = 16384
POOL_WINDOWS = (2, 4, 8, 16)
POOL_HIST = max(POOL_WINDOWS) - 1
HEAD = 128
SHORT_CONV = 3
FFN_CONV = 3

TM = 512
TF = 512
FFN_SPLIT = 2
VMEM_LIMIT = 56 * 1024 * 1024


def _rmsnorm(x, g):
    ms = jnp.mean(x * x, axis=-1, keepdims=True)
    return x * lax.rsqrt(ms + EPS) * g


def _gelu_tanh(x):
    c = 0.7978845608028654
    return x * (0.5 * (1.0 + jnp.tanh(c * (x + 0.044715 * (x * x * x)))))


def _params(*sem):
    return pltpu.CompilerParams(dimension_semantics=sem, vmem_limit_bytes=VMEM_LIMIT)


def _layer_spec(a, l, n_grid):
    zeros = (0,) * (a.ndim - 1)
    if n_grid == 1:
        return pl.BlockSpec((None,) + a.shape[1:], lambda i: (l,) + zeros)
    return pl.BlockSpec((None,) + a.shape[1:], lambda i, j: (l,) + zeros)


def _inproj_kernel(x_ref, g_ref, w_ref, o_ref, h_ref):
    @pl.when(pl.program_id(1) == 0)
    def _():
        h_ref[...] = _rmsnorm(x_ref[...], g_ref[...]).astype(BF16)

    o_ref[...] = jnp.dot(h_ref[...], w_ref[...], preferred_element_type=F32)


def _inproj(l, x, g, w):
    m, d = x.shape
    n = w.shape[2]
    tn = n // 2
    return pl.pallas_call(
        _inproj_kernel,
        out_shape=jax.ShapeDtypeStruct((m, n), F32),
        grid=(m // TM, n // tn),
        in_specs=[
            pl.BlockSpec((TM, d), lambda i, j: (i, 0)),
            _layer_spec(g, l, 2),
            pl.BlockSpec((None, d, tn), lambda i, j: (l, 0, j)),
        ],
        out_specs=pl.BlockSpec((TM, tn), lambda i, j: (i, j)),
        scratch_shapes=[pltpu.VMEM((TM, d), BF16)],
        compiler_params=_params("parallel", "arbitrary"),
        name="inproj",
    )(x, g, w)


def _mix_prompt_kernel(tiles_per_seq, widths, p_ref, pw_ref, ps_ref, cw_ref, sn_ref, sw_ref,
                       sb_ref, o_ref, zt_ref, zzp_ref, zzc_ref):
    pool_w, conv_w, sgu_w = widths
    o1 = pool_w
    o2 = o1 + conv_w
    o3 = o2 + conv_w
    o4 = o3 + conv_w
    o5 = o4 + sgu_w
    ph = 16
    ch = 8
    tile_in_seq = pl.program_id(0) % tiles_per_seq

    @pl.when(tile_in_seq == 0)
    def _():
        zzp_ref[0:ph, :] = jnp.zeros((ph, pool_w), F32)
        zzc_ref[0:ch, :] = jnp.zeros((ch, conv_w), F32)

    @pl.when(tile_in_seq != 0)
    def _():
        zzp_ref[0:ph, :] = zzp_ref[TM:TM + ph, :]
        zzc_ref[0:ch, :] = zzc_ref[TM:TM + ch, :]

    zzp_ref[ph:ph + TM, :] = p_ref[:, 0:o1]
    pos = tile_in_seq * TM + lax.broadcasted_iota(jnp.int32, (TM, 1), 0)
    for g, w in enumerate(POOL_WINDOWS):
        c0, c1 = g * HEAD, (g + 1) * HEAD
        xa = p_ref[:, c0:c1]
        s = xa
        for j in range(1, w):
            s = s + zzp_ref[ph - j:ph - j + TM, c0:c1]
        cnt = jnp.minimum(w, pos + 1).astype(F32)
        pooled = s / cnt - xa
        y = jnp.dot(pooled.astype(BF16), pw_ref[g], preferred_element_type=F32)
        o_ref[:, c0:c1] = (y * ps_ref[:, c0:c1]).astype(BF16)

    z = p_ref[:, o3:o4] * p_ref[:, o1:o2]
    zzc_ref[ch:ch + TM, :] = z
    y = (cw_ref[2:3, :] * z + cw_ref[1:2, :] * zzc_ref[ch - 1:ch - 1 + TM, :]
         + cw_ref[0:1, :] * zzc_ref[ch - 2:ch - 2 + TM, :])
    o_ref[:, o1:o2] = (p_ref[:, o2:o3] * y).astype(BF16)
    zt_ref[...] = zzc_ref[TM:TM + ch, :]

    vn = _rmsnorm(p_ref[:, o5:o5 + sgu_w], sn_ref[...]).astype(BF16)
    row = lax.broadcasted_iota(jnp.int32, (HEAD, HEAD), 0)
    col = lax.broadcasted_iota(jnp.int32, (HEAD, HEAD), 1)
    for h in range(sgu_w // HEAD):
        wh = jnp.where(row >= col, sw_ref[h], jnp.zeros((HEAD, HEAD), BF16))
        h0, h1 = h * HEAD, (h + 1) * HEAD
        for c in range(TM // HEAD):
            r0, r1 = c * HEAD, (c + 1) * HEAD
            gate = jnp.dot(wh, vn[r0:r1, h0:h1], preferred_element_type=F32) + sb_ref[:, h0:h1]
            o_ref[r0:r1, o2 + h0:o2 + h1] = (p_ref[r0:r1, o4 + h0:o4 + h1] * gate).astype(BF16)


def _mix_prompt(l, p, seq, widths, pool_w, pool_scale, conv_w, sgu_norm, sgu_w, sgu_bx):
    pw, cw, sw = widths
    n_rows = p.shape[0]
    n_tiles = n_rows // TM
    params = (pool_w, pool_scale, conv_w, sgu_norm, sgu_w, sgu_bx)
    return pl.pallas_call(
        functools.partial(_mix_prompt_kernel, seq // TM, widths),
        out_shape=(jax.ShapeDtypeStruct((n_rows, pw + cw + sw), BF16),
                   jax.ShapeDtypeStruct((n_tiles * 8, cw), F32)),
        grid=(n_tiles,),
        in_specs=[pl.BlockSpec((TM, p.shape[1]), lambda i: (i, 0))] + [_layer_spec(a, l, 1) for a in params],
        out_specs=(pl.BlockSpec((TM, pw + cw + sw), lambda i: (i, 0)),
                   pl.BlockSpec((8, cw), lambda i: (i, 0))),
        scratch_shapes=[pltpu.VMEM((16 + TM, pw), F32), pltpu.VMEM((8 + TM, cw), F32)],
        compiler_params=_params("arbitrary"),
        name="mix_prompt",
    )(p, *params)


def _mix_sample_kernel(dec_seq, widths, p_ref, sp_ref, sc_ref, pw_ref, ps_ref, cw_ref, sn_ref,
                       wx_ref, sb_ref, o_ref, zs_ref, vn_ref):
    pool_w, conv_w, sgu_w = widths
    o1 = pool_w
    o2 = o1 + conv_w
    o3 = o2 + conv_w
    o4 = o3 + conv_w
    o5 = o4 + sgu_w
    sb = TM // dec_seq
    slab = lambda t: slice(t * sb, (t + 1) * sb)

    def zz(k, c0, c1):
        if k < POOL_HIST:
            return sp_ref[slab(k), c0:c1]
        return p_ref[slab(k - POOL_HIST), c0:c1]

    for g, w in enumerate(POOL_WINDOWS):
        c0, c1 = g * HEAD, (g + 1) * HEAD
        for t in range(dec_seq):
            s = zz(POOL_HIST + t, c0, c1)
            for j in range(1, w):
                s = s + zz(POOL_HIST + t - j, c0, c1)
            cnt = float(min(w, PAST_LEN + t + 1))
            pooled = s / cnt - p_ref[slab(t), c0:c1]
            y = jnp.dot(pooled.astype(BF16), pw_ref[g], preferred_element_type=F32)
            o_ref[slab(t), c0:c1] = (y * ps_ref[:, c0:c1]).astype(BF16)

    def zc(k):
        if k < SHORT_CONV - 1:
            return sc_ref[slab(k), :]
        t = k - (SHORT_CONV - 1)
        return p_ref[slab(t), o3:o4] * p_ref[slab(t), o1:o2]

    for t in range(dec_seq):
        y = cw_ref[0:1, :] * zc(t) + cw_ref[1:2, :] * zc(t + 1) + cw_ref[2:3, :] * zc(t + 2)
        o_ref[slab(t), o1:o2] = (p_ref[slab(t), o2:o3] * y).astype(BF16)
    for k in range(SHORT_CONV - 1):
        zs_ref[slab(k), :] = zc(dec_seq + k)

    vn_ref[...] = _rmsnorm(p_ref[:, o5:o5 + sgu_w], sn_ref[...])
    for t in range(dec_seq):
        gate = sb_ref[t:t + 1, :] + wx_ref[t * dec_seq:t * dec_seq + 1, :] * vn_ref[slab(0), :]
        for s in range(1, t + 1):
            gate = gate + wx_ref[t * dec_seq + s:t * dec_seq + s + 1, :] * vn_ref[slab(s), :]
        o_ref[slab(t), o2:o2 + sgu_w] = (p_ref[slab(t), o4:o5] * gate).astype(BF16)


def _mix_sample(l, p, dec_seq, widths, st_pool, st_conv, pool_w, pool_scale, conv_w, sgu_norm, sgu_wx,
                sgu_bx):
    pw, cw, sw = widths
    n_rows = p.shape[0]
    n_tiles = n_rows // TM
    sb = TM // dec_seq
    params = (pool_w, pool_scale, conv_w, sgu_norm, sgu_wx, sgu_bx)
    return pl.pallas_call(
        functools.partial(_mix_sample_kernel, dec_seq, widths),
        out_shape=(jax.ShapeDtypeStruct((n_rows, pw + cw + sw), BF16),
                   jax.ShapeDtypeStruct((n_tiles * (SHORT_CONV - 1) * sb, cw), F32),
                   jax.ShapeDtypeStruct((n_rows, sw), F32)),
        grid=(n_tiles,),
        in_specs=[pl.BlockSpec((TM, p.shape[1]), lambda i: (i, 0)),
                  pl.BlockSpec((None, POOL_HIST * sb, pw), lambda i: (l, i, 0)),
                  pl.BlockSpec((None, (SHORT_CONV - 1) * sb, cw), lambda i: (l, i, 0))]
        + [_layer_spec(a, l, 1) for a in params],
        out_specs=(pl.BlockSpec((TM, pw + cw + sw), lambda i: (i, 0)),
                   pl.BlockSpec(((SHORT_CONV - 1) * sb, cw), lambda i: (i, 0)),
                   pl.BlockSpec((TM, sw), lambda i: (i, 0))),
        compiler_params=_params("parallel"),
        name="mix_sample",
    )(p, st_pool, st_conv, *params)


def _outproj_kernel(mix_ref, w_ref, x_ref, g_ref, o_ref):
    a = jnp.dot(mix_ref[...], w_ref[...], preferred_element_type=F32)
    o_ref[...] = x_ref[...] + _rmsnorm(a, g_ref[...])


def _outproj(l, mix, w, x, g):
    m, d = x.shape
    k = mix.shape[1]
    return pl.pallas_call(
        _outproj_kernel,
        out_shape=jax.ShapeDtypeStruct((m, d), F32),
        grid=(m // TM,),
        in_specs=[pl.BlockSpec((TM, k), lambda i: (i, 0)),
                  _layer_spec(w, l, 1),
                  pl.BlockSpec((TM, d), lambda i: (i, 0)),
                  _layer_spec(g, l, 1)],
        out_specs=pl.BlockSpec((TM, d), lambda i: (i, 0)),
        compiler_params=_params("parallel"),
        name="outproj",
    )(mix, w, x, g)


def _ffn_kernel(shift, tiles_per_seq, has_prev, *refs):
    if has_prev:
        (x_ref, gpre_ref, upg_ref, upv_ref, cw_ref, cb_ref, dn_ref, gpost_ref, prev_ref,
         o_ref, gt_ref, hn_ref, acc_ref, gbuf_ref) = refs
    else:
        (x_ref, gpre_ref, upg_ref, upv_ref, cw_ref, cb_ref, dn_ref, gpost_ref,
         o_ref, gt_ref, hn_ref, acc_ref, gbuf_ref, carry_ref) = refs
    f = pl.program_id(1)
    nf = pl.num_programs(1)
    hist = gbuf_ref.shape[0] - TM
    keep = gt_ref.shape[0]

    @pl.when(f == 0)
    def _():
        hn_ref[...] = _rmsnorm(x_ref[...], gpre_ref[...]).astype(BF16)
        acc_ref[...] = jnp.zeros(acc_ref.shape, F32)

    if has_prev:
        gbuf_ref[0:hist, :] = prev_ref[...]
    else:
        tile_in_seq = pl.program_id(0) % tiles_per_seq

        @pl.when(tile_in_seq == 0)
        def _():
            gbuf_ref[0:hist, :] = jnp.zeros((hist, TF), F32)

        @pl.when(tile_in_seq != 0)
        def _():
            gbuf_ref[0:hist, :] = carry_ref[f]

    cols = TF // FFN_SPLIT
    total = None
    for c in range(FFN_SPLIT):
        cs = slice(c * cols, (c + 1) * cols)
        gate = jnp.dot(hn_ref[...], upg_ref[:, cs].astype(BF16), preferred_element_type=F32)
        val = jnp.dot(hn_ref[...], upv_ref[:, cs].astype(BF16), preferred_element_type=F32)
        gbuf_ref[hist:hist + TM, cs] = gate
        gc = (cw_ref[2:3, cs] * gate
              + cw_ref[1:2, cs] * gbuf_ref[hist - shift:hist - shift + TM, cs]
              + cw_ref[0:1, cs] * gbuf_ref[hist - 2 * shift:hist - 2 * shift + TM, cs]
              + cb_ref[:, cs])
        h = (_gelu_tanh(gc) * val).astype(BF16)
        part = jnp.dot(h, dn_ref[cs, :].astype(BF16), preferred_element_type=F32)
        total = part if total is None else total + part
    acc_ref[...] += total

    tail = gbuf_ref[hist + TM - keep:hist + TM, :]
    gt_ref[...] = tail
    if not has_prev:
        carry_ref[f] = tail

    @pl.when(f == nf - 1)
    def _():
        o_ref[...] = x_ref[...] + _rmsnorm(acc_ref[...], gpost_ref[...])


def _ffn(l, x, g_pre, up, conv_w, conv_b, down, g_post, *, seq=None, dec_seq=None, prev=None):
    m, d = x.shape
    dff = down.shape[1]
    nf = dff // TF
    n_tiles = m // TM
    sample = prev is not None
    if sample:
        shift = TM // dec_seq
        hist = keep = (FFN_CONV - 1) * shift
        tiles_per_seq = None
    else:
        shift = 1
        hist = keep = 8
        tiles_per_seq = seq // TM
    in_specs = [pl.BlockSpec((TM, d), lambda i, j: (i, 0)),
                _layer_spec(g_pre, l, 2),
                pl.BlockSpec((None, d, TF), lambda i, j: (l, 0, j)),
                pl.BlockSpec((None, d, TF), lambda i, j: (l, 0, nf + j)),
                pl.BlockSpec((None, FFN_CONV, TF), lambda i, j: (l, 0, j)),
                pl.BlockSpec((None, 1, TF), lambda i, j: (l, 0, j)),
                pl.BlockSpec((None, TF, d), lambda i, j: (l, j, 0)),
                _layer_spec(g_post, l, 2)]
    args = [x, g_pre, up, up, conv_w, conv_b, down, g_post]
    scratch = [pltpu.VMEM((TM, d), BF16), pltpu.VMEM((TM, d), F32), pltpu.VMEM((hist + TM, TF), F32)]
    if sample:
        in_specs.append(pl.BlockSpec((None, hist, TF), lambda i, j: (l, i, j)))
        args.append(prev)
    else:
        scratch.append(pltpu.VMEM((nf, keep, TF), F32))
    return pl.pallas_call(
        functools.partial(_ffn_kernel, shift, tiles_per_seq, sample),
        out_shape=(jax.ShapeDtypeStruct((m, d), F32),
                   jax.ShapeDtypeStruct((n_tiles * keep, dff), F32)),
        grid=(n_tiles, nf),
        in_specs=in_specs,
        out_specs=(pl.BlockSpec((TM, d), lambda i, j: (i, 0)),
                   pl.BlockSpec((keep, TF), lambda i, j: (i, j))),
        scratch_shapes=scratch,
        compiler_params=_params("arbitrary", "arbitrary"),
        name="ffn_sample" if sample else "ffn_prompt",
    )(*args)


def _to_tiles(a, sb):
    *lead, b, t, c = a.shape
    n = len(lead)
    a = a.reshape(*lead, b // sb, sb, t, c)
    a = jnp.swapaxes(a, n + 1, n + 2)
    return a.reshape(*lead, b * t, c)


def _from_tiles(a, t, sb):
    c = a.shape[-1]
    n = a.shape[0] // (t * sb)
    return a.reshape(n, t, sb, c).transpose(0, 2, 1, 3).reshape(n * sb, t, c)


def kernel(x_prompt, x_sample, state_pool, state_conv, state_ffn, g_mix_pre, g_mix_post, g_ffn_pre,
           g_ffn_post, w_in, pool_w, pool_scale, conv_w, sgu_norm, sgu_w, sgu_b, w_out, ffn_up,
           ffn_conv_w, ffn_conv_b, ffn_down):
    batch, seq, d = x_prompt.shape
    dec_batch, dec_seq, _ = x_sample.shape
    depth = w_in.shape[0]
    pw = pool_scale.shape[1]
    cw = conv_w.shape[2]
    sw = sgu_norm.shape[1]
    widths = (pw, cw, sw)
    assert seq % TM == 0 and TM % dec_seq == 0 and TM % HEAD == 0
    assert PAST_LEN % HEAD == 0 and dec_seq <= HEAD
    sb = TM // dec_seq
    assert dec_batch % sb == 0
    tps = seq // TM

    xp = x_prompt.reshape(batch * seq, d)
    xs = _to_tiles(x_sample, sb)
    st_pool = _to_tiles(state_pool, sb)
    st_conv = _to_tiles(state_conv, sb)
    st_ffn = _to_tiles(state_ffn, sb)

    vec = lambda a: a[:, None, :]
    g_mix_pre, g_mix_post, g_ffn_pre, g_ffn_post = map(vec, (g_mix_pre, g_mix_post, g_ffn_pre, g_ffn_post))
    pool_scale, sgu_norm, ffn_conv_b = map(vec, (pool_scale, sgu_norm, ffn_conv_b))
    w_in_b = w_in.astype(BF16)
    w_out_b = w_out.astype(BF16)
    pool_w_b = pool_w.astype(BF16)
    sgu_w_b = sgu_w.astype(BF16)
    sgu_bx = jnp.repeat(jnp.swapaxes(sgu_b, 1, 2), HEAD, axis=2)
    sgu_wx = jnp.repeat(jnp.transpose(sgu_w[:, :, :dec_seq, :dec_seq], (0, 2, 3, 1)), HEAD, axis=3)
    sgu_wx = sgu_wx.reshape(depth, dec_seq * dec_seq, sw)

    outs = {k: [] for k in ("pool_p", "pool_s", "conv_p", "conv_s", "ffn_p", "ffn_s", "v_s")}
    for l in range(depth):
        pp = _inproj(l, xp, g_mix_pre, w_in_b)
        ps = _inproj(l, xs, g_mix_pre, w_in_b)
        mix_p, zt = _mix_prompt(l, pp, seq, widths, pool_w_b, pool_scale, conv_w, sgu_norm, sgu_w_b, sgu_bx)
        mix_s, zs, vn = _mix_sample(l, ps, dec_seq, widths, st_pool, st_conv, pool_w_b, pool_scale, conv_w,
                                    sgu_norm, sgu_wx, sgu_bx)
        xp = _outproj(l, mix_p, w_out_b, xp, g_mix_post)
        xs = _outproj(l, mix_s, w_out_b, xs, g_mix_post)
        xp, gt_p = _ffn(l, xp, g_ffn_pre, ffn_up, ffn_conv_w, ffn_conv_b, ffn_down, g_ffn_post, seq=seq)
        xs, gt_s = _ffn(l, xs, g_ffn_pre, ffn_up, ffn_conv_w, ffn_conv_b, ffn_down, g_ffn_post,
                        dec_seq=dec_seq, prev=st_ffn)

        xa_s = _from_tiles(ps[:, :pw], dec_seq, sb)
        outs["pool_p"].append(pp.reshape(batch, seq, -1)[:, seq - POOL_HIST:, :pw])
        outs["pool_s"].append(jnp.concatenate([state_pool[l], xa_s], axis=1)[:, -POOL_HIST:])
        outs["conv_p"].append(zt.reshape(batch, tps, 8, cw)[:, -1, 8 - (SHORT_CONV - 1):])
        outs["conv_s"].append(_from_tiles(zs, SHORT_CONV - 1, sb))
        outs["ffn_p"].append(gt_p.reshape(batch, tps, 8, -1)[:, -1, 8 - (FFN_CONV - 1):])
        outs["ffn_s"].append(_from_tiles(gt_s, FFN_CONV - 1, sb))
        outs["v_s"].append(_from_tiles(vn, dec_seq, sb))

    st = lambda k: jnp.stack(outs[k])
    return (xp.reshape(batch, seq, d), _from_tiles(xs, dec_seq, sb), st("pool_p"), st("pool_s"),
            st("conv_p"), st("conv_s"), st("ffn_p"), st("ffn_s"), st("v_s"))
```

```python
import functools

import jax
import jax.numpy as jnp
from jax import lax
from jax.experimental import pallas as pl
from jax.experimental.pallas import tpu as pltpu

F32 = jnp.float32
BF16 = jnp.bfloat16

EPS = 1e-6
PAST_LEN = 16384
POOL_WINDOWS = (2, 4, 8, 16)
POOL_HIST = max(POOL_WINDOWS) - 1
HEAD = 128
SHORT_CONV = 3
FFN_CONV = 3

TM = 512
TF = 512
FFN_SPLIT = 2
VMEM_LIMIT = 56 * 1024 * 1024


def _rmsnorm(x, g):
    ms = jnp.mean(x * x, axis=-1, keepdims=True)
    return x * lax.rsqrt(ms + EPS) * g


def _gelu_tanh(x):
    c = 0.7978845608028654
    return x * (0.5 * (1.0 + jnp.tanh(c * (x + 0.044715 * (x * x * x)))))


def _params(*sem):
    return pltpu.CompilerParams(dimension_semantics=sem, vmem_limit_bytes=VMEM_LIMIT)


def _layer_spec(a, l, n_grid):
    zeros = (0,) * (a.ndim - 1)
    if n_grid == 1:
        return pl.BlockSpec((None,) + a.shape[1:], lambda i: (l,) + zeros)
    return pl.BlockSpec((None,) + a.shape[1:], lambda i, j: (l,) + zeros)


def _inproj_kernel(x_ref, g_ref, w_ref, o_ref):
    h = _rmsnorm(x_ref[...], g_ref[...]).astype(BF16)
    o_ref[...] = jnp.dot(h, w_ref[...], preferred_element_type=F32)


def _inproj(l, x, g, w):
    m, d = x.shape
    n = w.shape[2]
    tn = n // 2
    return pl.pallas_call(
        _inproj_kernel,
        out_shape=jax.ShapeDtypeStruct((m, n), F32),
        grid=(n // tn, m // TM),
        in_specs=[
            pl.BlockSpec((TM, d), lambda j, i: (i, 0)),
            _layer_spec(g, l, 2),
            pl.BlockSpec((None, d, tn), lambda j, i: (l, 0, j)),
        ],
        out_specs=pl.BlockSpec((TM, tn), lambda j, i: (i, j)),
        compiler_params=_params("parallel", "parallel"),
        name="inproj",
    )(x, g, w)


def _mix_prompt_kernel(tiles_per_seq, widths, p_ref, pw_ref, ps_ref, cw_ref, sn_ref, sw_ref,
                       sb_ref, o_ref, zt_ref, zzp_ref, zzc_ref):
    pool_w, conv_w, sgu_w = widths
    o1 = pool_w
    o2 = o1 + conv_w
    o3 = o2 + conv_w
    o4 = o3 + conv_w
    o5 = o4 + sgu_w
    ph = 16
    ch = 8
    tile_in_seq = pl.program_id(0) % tiles_per_seq

    @pl.when(tile_in_seq == 0)
    def _():
        zzp_ref[0:ph, :] = jnp.zeros((ph, pool_w), F32)
        zzc_ref[0:ch, :] = jnp.zeros((ch, conv_w), F32)

    @pl.when(tile_in_seq != 0)
    def _():
        zzp_ref[0:ph, :] = zzp_ref[TM:TM + ph, :]
        zzc_ref[0:ch, :] = zzc_ref[TM:TM + ch, :]

    zzp_ref[ph:ph + TM, :] = p_ref[:, 0:o1]
    pos = tile_in_seq * TM + lax.broadcasted_iota(jnp.int32, (TM, 1), 0)
    for g, w in enumerate(POOL_WINDOWS):
        c0, c1 = g * HEAD, (g + 1) * HEAD
        xa = p_ref[:, c0:c1]
        s = xa
        for j in range(1, w):
            s = s + zzp_ref[ph - j:ph - j + TM, c0:c1]
        cnt = jnp.minimum(w, pos + 1).astype(F32)
        pooled = s / cnt - xa
        y = jnp.dot(pooled.astype(BF16), pw_ref[g], preferred_element_type=F32)
        o_ref[:, c0:c1] = (y * ps_ref[:, c0:c1]).astype(BF16)

    z = p_ref[:, o3:o4] * p_ref[:, o1:o2]
    zzc_ref[ch:ch + TM, :] = z
    y = (cw_ref[2:3, :] * z + cw_ref[1:2, :] * zzc_ref[ch - 1:ch - 1 + TM, :]
         + cw_ref[0:1, :] * zzc_ref[ch - 2:ch - 2 + TM, :])
    o_ref[:, o1:o2] = (p_ref[:, o2:o3] * y).astype(BF16)
    zt_ref[...] = zzc_ref[TM:TM + ch, :]

    vn = _rmsnorm(p_ref[:, o5:o5 + sgu_w], sn_ref[...]).astype(BF16)
    row = lax.broadcasted_iota(jnp.int32, (HEAD, HEAD), 0)
    col = lax.broadcasted_iota(jnp.int32, (HEAD, HEAD), 1)
    for h in range(sgu_w // HEAD):
        wh = jnp.where(row >= col, sw_ref[h], jnp.zeros((HEAD, HEAD), BF16))
        h0, h1 = h * HEAD, (h + 1) * HEAD
        for c in range(TM // HEAD):
            r0, r1 = c * HEAD, (c + 1) * HEAD
            gate = jnp.dot(wh, vn[r0:r1, h0:h1], preferred_element_type=F32) + sb_ref[:, h0:h1]
            o_ref[r0:r1, o2 + h0:o2 + h1] = (p_ref[r0:r1, o4 + h0:o4 + h1] * gate).astype(BF16)


def _mix_prompt(l, p, seq, widths, pool_w, pool_scale, conv_w, sgu_norm, sgu_w, sgu_bx):
    pw, cw, sw = widths
    n_rows = p.shape[0]
    n_tiles = n_rows // TM
    params = (pool_w, pool_scale, conv_w, sgu_norm, sgu_w, sgu_bx)
    return pl.pallas_call(
        functools.partial(_mix_prompt_kernel, seq // TM, widths),
        out_shape=(jax.ShapeDtypeStruct((n_rows, pw + cw + sw), BF16),
                   jax.ShapeDtypeStruct((n_tiles * 8, cw), F32)),
        grid=(n_tiles,),
        in_specs=[pl.BlockSpec((TM, p.shape[1]), lambda i: (i, 0))] + [_layer_spec(a, l, 1) for a in params],
        out_specs=(pl.BlockSpec((TM, pw + cw + sw), lambda i: (i, 0)),
                   pl.BlockSpec((8, cw), lambda i: (i, 0))),
        scratch_shapes=[pltpu.VMEM((16 + TM, pw), F32), pltpu.VMEM((8 + TM, cw), F32)],
        compiler_params=_params("arbitrary"),
        name="mix_prompt",
    )(p, *params)


def _mix_sample_kernel(dec_seq, widths, p_ref, sp_ref, sc_ref, pw_ref, ps_ref, cw_ref, sn_ref,
                       wx_ref, sb_ref, o_ref, zs_ref, vn_ref):
    pool_w, conv_w, sgu_w = widths
    o1 = pool_w
    o2 = o1 + conv_w
    o3 = o2 + conv_w
    o4 = o3 + conv_w
    o5 = o4 + sgu_w
    sb = TM // dec_seq
    slab = lambda t: slice(t * sb, (t + 1) * sb)

    def zz(k, c0, c1):
        if k < POOL_HIST:
            return sp_ref[slab(k), c0:c1]
        return p_ref[slab(k - POOL_HIST), c0:c1]

    for g, w in enumerate(POOL_WINDOWS):
        c0, c1 = g * HEAD, (g + 1) * HEAD
        for t in range(dec_seq):
            s = zz(POOL_HIST + t, c0, c1)
            for j in range(1, w):
                s = s + zz(POOL_HIST + t - j, c0, c1)
            cnt = float(min(w, PAST_LEN + t + 1))
            pooled = s / cnt - p_ref[slab(t), c0:c1]
            y = jnp.dot(pooled.astype(BF16), pw_ref[g], preferred_element_type=F32)
            o_ref[slab(t), c0:c1] = (y * ps_ref[:, c0:c1]).astype(BF16)

    def zc(k):
        if k < SHORT_CONV - 1:
            return sc_ref[slab(k), :]
        t = k - (SHORT_CONV - 1)
        return p_ref[slab(t), o3:o4] * p_ref[slab(t), o1:o2]

    for t in range(dec_seq):
        y = cw_ref[0:1, :] * zc(t) + cw_ref[1:2, :] * zc(t + 1) + cw_ref[2:3, :] * zc(t + 2)
        o_ref[slab(t), o1:o2] = (p_ref[slab(t), o2:o3] * y).astype(BF16)
    for k in range(SHORT_CONV - 1):
        zs_ref[slab(k), :] = zc(dec_seq + k)

    vn_ref[...] = _rmsnorm(p_ref[:, o5:o5 + sgu_w], sn_ref[...])
    for t in range(dec_seq):
        gate = sb_ref[t:t + 1, :] + wx_ref[t * dec_seq:t * dec_seq + 1, :] * vn_ref[slab(0), :]
        for s in range(1, t + 1):
            gate = gate + wx_ref[t * dec_seq + s:t * dec_seq + s + 1, :] * vn_ref[slab(s), :]
        o_ref[slab(t), o2:o2 + sgu_w] = (p_ref[slab(t), o4:o5] * gate).astype(BF16)


def _mix_sample(l, p, dec_seq, widths, st_pool, st_conv, pool_w, pool_scale, conv_w, sgu_norm, sgu_wx,
                sgu_bx):
    pw, cw, sw = widths
    n_rows = p.shape[0]
    n_tiles = n_rows // TM
    sb = TM // dec_seq
    params = (pool_w, pool_scale, conv_w, sgu_norm, sgu_wx, sgu_bx)
    return pl.pallas_call(
        functools.partial(_mix_sample_kernel, dec_seq, widths),
        out_shape=(jax.ShapeDtypeStruct((n_rows, pw + cw + sw), BF16),
                   jax.ShapeDtypeStruct((n_tiles * (SHORT_CONV - 1) * sb, cw), F32),
                   jax.ShapeDtypeStruct((n_rows, sw), F32)),
        grid=(n_tiles,),
        in_specs=[pl.BlockSpec((TM, p.shape[1]), lambda i: (i, 0)),
                  pl.BlockSpec((None, POOL_HIST * sb, pw), lambda i: (l, i, 0)),
                  pl.BlockSpec((None, (SHORT_CONV - 1) * sb, cw), lambda i: (l, i, 0))]
        + [_layer_spec(a, l, 1) for a in params],
        out_specs=(pl.BlockSpec((TM, pw + cw + sw), lambda i: (i, 0)),
                   pl.BlockSpec(((SHORT_CONV - 1) * sb, cw), lambda i: (i, 0)),
                   pl.BlockSpec((TM, sw), lambda i: (i, 0))),
        compiler_params=_params("parallel"),
        name="mix_sample",
    )(p, st_pool, st_conv, *params)


def _outproj_kernel(mix_ref, w_ref, x_ref, g_ref, o_ref):
    a = jnp.dot(mix_ref[...], w_ref[...], preferred_element_type=F32)
    o_ref[...] = x_ref[...] + _rmsnorm(a, g_ref[...])


def _outproj(l, mix, w, x, g):
    m, d = x.shape
    k = mix.shape[1]
    return pl.pallas_call(
        _outproj_kernel,
        out_shape=jax.ShapeDtypeStruct((m, d), F32),
        grid=(m // TM,),
        in_specs=[pl.BlockSpec((TM, k), lambda i: (i, 0)),
                  _layer_spec(w, l, 1),
                  pl.BlockSpec((TM, d), lambda i: (i, 0)),
                  _layer_spec(g, l, 1)],
        out_specs=pl.BlockSpec((TM, d), lambda i: (i, 0)),
        compiler_params=_params("parallel"),
        name="outproj",
    )(mix, w, x, g)


def _ffn_kernel(shift, tiles_per_seq, has_prev, *refs):
    if has_prev:
        (x_ref, gpre_ref, upg_ref, upv_ref, cw_ref, cb_ref, dn_ref, gpost_ref, prev_ref,
         o_ref, gt_ref, hn_ref, acc_ref, gbuf_ref) = refs
    else:
        (x_ref, gpre_ref, upg_ref, upv_ref, cw_ref, cb_ref, dn_ref, gpost_ref,
         o_ref, gt_ref, hn_ref, acc_ref, gbuf_ref, carry_ref) = refs
    f = pl.program_id(1)
    nf = pl.num_programs(1)
    hist = gbuf_ref.shape[0] - TM
    keep = gt_ref.shape[0]

    @pl.when(f == 0)
    def _():
        hn_ref[...] = _rmsnorm(x_ref[...], gpre_ref[...]).astype(BF16)
        acc_ref[...] = jnp.zeros(acc_ref.shape, F32)

    if has_prev:
        gbuf_ref[0:hist, :] = prev_ref[...]
    else:
        tile_in_seq = pl.program_id(0) % tiles_per_seq

        @pl.when(tile_in_seq == 0)
        def _():
            gbuf_ref[0:hist, :] = jnp.zeros((hist, TF), F32)

        @pl.when(tile_in_seq != 0)
        def _():
            gbuf_ref[0:hist, :] = carry_ref[f]

    cols = TF // FFN_SPLIT
    total = None
    for c in range(FFN_SPLIT):
        cs = slice(c * cols, (c + 1) * cols)
        gate = jnp.dot(hn_ref[...], upg_ref[:, cs], preferred_element_type=F32)
        val = jnp.dot(hn_ref[...], upv_ref[:, cs], preferred_element_type=F32)
        gbuf_ref[hist:hist + TM, cs] = gate
        gc = (cw_ref[2:3, cs] * gate
              + cw_ref[1:2, cs] * gbuf_ref[hist - shift:hist - shift + TM, cs]
              + cw_ref[0:1, cs] * gbuf_ref[hist - 2 * shift:hist - 2 * shift + TM, cs]
              + cb_ref[:, cs])
        h = (_gelu_tanh(gc) * val).astype(BF16)
        part = jnp.dot(h, dn_ref[cs, :].astype(BF16), preferred_element_type=F32)
        total = part if total is None else total + part
    acc_ref[...] += total

    tail = gbuf_ref[hist + TM - keep:hist + TM, :]
    gt_ref[...] = tail
    if not has_prev:
        carry_ref[f] = tail

    @pl.when(f == nf - 1)
    def _():
        o_ref[...] = x_ref[...] + _rmsnorm(acc_ref[...], gpost_ref[...])


def _ffn(l, x, g_pre, up, conv_w, conv_b, down, g_post, *, seq=None, dec_seq=None, prev=None):
    m, d = x.shape
    dff = down.shape[1]
    nf = dff // TF
    n_tiles = m // TM
    sample = prev is not None
    if sample:
        shift = TM // dec_seq
        hist = keep = (FFN_CONV - 1) * shift
        tiles_per_seq = None
    else:
        shift = 1
        hist = keep = 8
        tiles_per_seq = seq // TM
    in_specs = [pl.BlockSpec((TM, d), lambda i, j: (i, 0)),
                _layer_spec(g_pre, l, 2),
                pl.BlockSpec((None, d, TF), lambda i, j: (l, 0, j)),
                pl.BlockSpec((None, d, TF), lambda i, j: (l, 0, nf + j)),
                pl.BlockSpec((None, FFN_CONV, TF), lambda i, j: (l, 0, j)),
                pl.BlockSpec((None, 1, TF), lambda i, j: (l, 0, j)),
                pl.BlockSpec((None, TF, d), lambda i, j: (l, j, 0)),
                _layer_spec(g_post, l, 2)]
    args = [x, g_pre, up, up, conv_w, conv_b, down, g_post]
    scratch = [pltpu.VMEM((TM, d), BF16), pltpu.VMEM((TM, d), F32), pltpu.VMEM((hist + TM, TF), F32)]
    if sample:
        in_specs.append(pl.BlockSpec((None, hist, TF), lambda i, j: (l, i, j)))
        args.append(prev)
    else:
        scratch.append(pltpu.VMEM((nf, keep, TF), F32))
    return pl.pallas_call(
        functools.partial(_ffn_kernel, shift, tiles_per_seq, sample),
        out_shape=(jax.ShapeDtypeStruct((m, d), F32),
                   jax.ShapeDtypeStruct((n_tiles * keep, dff), F32)),
        grid=(n_tiles, nf),
        in_specs=in_specs,
        out_specs=(pl.BlockSpec((TM, d), lambda i, j: (i, 0)),
                   pl.BlockSpec((keep, TF), lambda i, j: (i, j))),
        scratch_shapes=scratch,
        compiler_params=_params("arbitrary", "arbitrary"),
        name="ffn_sample" if sample else "ffn_prompt",
    )(*args)


def _to_tiles(a, sb):
    *lead, b, t, c = a.shape
    n = len(lead)
    a = a.reshape(*lead, b // sb, sb, t, c)
    a = jnp.swapaxes(a, n + 1, n + 2)
    return a.reshape(*lead, b * t, c)


def _from_tiles(a, t, sb):
    c = a.shape[-1]
    n = a.shape[0] // (t * sb)
    return a.reshape(n, t, sb, c).transpose(0, 2, 1, 3).reshape(n * sb, t, c)


def kernel(x_prompt, x_sample, state_pool, state_conv, state_ffn, g_mix_pre, g_mix_post, g_ffn_pre,
           g_ffn_post, w_in, pool_w, pool_scale, conv_w, sgu_norm, sgu_w, sgu_b, w_out, ffn_up,
           ffn_conv_w, ffn_conv_b, ffn_down):
    batch, seq, d = x_prompt.shape
    dec_batch, dec_seq, _ = x_sample.shape
    depth = w_in.shape[0]
    pw = pool_scale.shape[1]
    cw = conv_w.shape[2]
    sw = sgu_norm.shape[1]
    widths = (pw, cw, sw)
    assert seq % TM == 0 and TM % dec_seq == 0 and TM % HEAD == 0
    assert PAST_LEN % HEAD == 0 and dec_seq <= HEAD
    sb = TM // dec_seq
    assert dec_batch % sb == 0
    tps = seq // TM

    xp = x_prompt.reshape(batch * seq, d)
    xs = _to_tiles(x_sample, sb)
    st_pool = _to_tiles(state_pool, sb)
    st_conv = _to_tiles(state_conv, sb)
    st_ffn = _to_tiles(state_ffn, sb)

    vec = lambda a: a[:, None, :]
    g_mix_pre, g_mix_post, g_ffn_pre, g_ffn_post = map(vec, (g_mix_pre, g_mix_post, g_ffn_pre, g_ffn_post))
    pool_scale, sgu_norm, ffn_conv_b = map(vec, (pool_scale, sgu_norm, ffn_conv_b))
    w_in_b = w_in.astype(BF16)
    w_out_b = w_out.astype(BF16)
    up_b = ffn_up.astype(BF16)
    pool_w_b = pool_w.astype(BF16)
    sgu_w_b = sgu_w.astype(BF16)
    sgu_bx = jnp.repeat(jnp.swapaxes(sgu_b, 1, 2), HEAD, axis=2)
    sgu_wx = jnp.repeat(jnp.transpose(sgu_w[:, :, :dec_seq, :dec_seq], (0, 2, 3, 1)), HEAD, axis=3)
    sgu_wx = sgu_wx.reshape(depth, dec_seq * dec_seq, sw)

    outs = {k: [] for k in ("pool_p", "pool_s", "conv_p", "conv_s", "ffn_p", "ffn_s", "v_s")}
    for l in range(depth):
        pp = _inproj(l, xp, g_mix_pre, w_in_b)
        ps = _inproj(l, xs, g_mix_pre, w_in_b)
        mix_p, zt = _mix_prompt(l, pp, seq, widths, pool_w_b, pool_scale, conv_w, sgu_norm, sgu_w_b, sgu_bx)
        mix_s, zs, vn = _mix_sample(l, ps, dec_seq, widths, st_pool, st_conv, pool_w_b, pool_scale, conv_w,
                                    sgu_norm, sgu_wx, sgu_bx)
        xp = _outproj(l, mix_p, w_out_b, xp, g_mix_post)
        xs = _outproj(l, mix_s, w_out_b, xs, g_mix_post)
        xp, gt_p = _ffn(l, xp, g_ffn_pre, up_b, ffn_conv_w, ffn_conv_b, ffn_down, g_ffn_post, seq=seq)
        xs, gt_s = _ffn(l, xs, g_ffn_pre, up_b, ffn_conv_w, ffn_conv_b, ffn_down, g_ffn_post,
                        dec_seq=dec_seq, prev=st_ffn)

        xa_s = _from_tiles(ps[:, :pw], dec_seq, sb)
        outs["pool_p"].append(pp.reshape(batch, seq, -1)[:, seq - POOL_HIST:, :pw])
        outs["pool_s"].append(jnp.concatenate([state_pool[l], xa_s], axis=1)[:, -POOL_HIST:])
        outs["conv_p"].append(zt.reshape(batch, tps, 8, cw)[:, -1, 8 - (SHORT_CONV - 1):])
        outs["conv_s"].append(_from_tiles(zs, SHORT_CONV - 1, sb))
        outs["ffn_p"].append(gt_p.reshape(batch, tps, 8, -1)[:, -1, 8 - (FFN_CONV - 1):])
        outs["ffn_s"].append(_from_tiles(gt_s, FFN_CONV - 1, sb))
        outs["v_s"].append(_from_tiles(vn, dec_seq, sb))

    st = lambda k: jnp.stack(outs[k])
    return (xp.reshape(batch, seq, d), _from_tiles(xs, dec_seq, sb), st("pool_p"), st("pool_s"),
            st("conv_p"), st("conv_s"), st("ffn_p"), st("ffn_s"), st("v_s"))
```

```python
import functools

import jax
import jax.numpy as jnp
from jax import lax
from jax.experimental import pallas as pl
from jax.experimental.pallas import tpu as pltpu

F32 = jnp.float32
BF16 = jnp.bfloat16

EPS = 1e-6
PAST_LEN = 16384
POOL_WINDOWS = (2, 4, 8, 16)
POOL_HIST = max(POOL_WINDOWS) - 1
HEAD = 128
SHORT_CONV = 3
FFN_CONV = 3

TM = 512
TF = 512
FFN_SPLIT = 2
VMEM_LIMIT = 56 * 1024 * 1024


def _rmsnorm(x, g):
    ms = jnp.mean(x * x, axis=-1, keepdims=True)
    return x * lax.rsqrt(ms + EPS) * g


def _gelu_tanh(x):
    c = 0.7978845608028654
    return x * (0.5 * (1.0 + jnp.tanh(c * (x + 0.044715 * (x * x * x)))))


def _params(*sem):
    return pltpu.CompilerParams(dimension_semantics=sem, vmem_limit_bytes=VMEM_LIMIT)


def _layer_spec(a, l, n_grid):
    zeros = (0,) * (a.ndim - 1)
    if n_grid == 1:
        return pl.BlockSpec((None,) + a.shape[1:], lambda i: (l,) + zeros)
    return pl.BlockSpec((None,) + a.shape[1:], lambda i, j: (l,) + zeros)


def _inproj_kernel(x_ref, g_ref, w_ref, o_ref):
    h = _rmsnorm(x_ref[...], g_ref[...]).astype(BF16)
    o_ref[...] = jnp.dot(h, w_ref[...], preferred_element_type=F32)


def _inproj(l, x, g, w):
    m, d = x.shape
    n = w.shape[2]
    tn = n // 2
    return pl.pallas_call(
        _inproj_kernel,
        out_shape=jax.ShapeDtypeStruct((m, n), F32),
        grid=(n // tn, m // TM),
        in_specs=[
            pl.BlockSpec((TM, d), lambda j, i: (i, 0)),
            _layer_spec(g, l, 2),
            pl.BlockSpec((None, d, tn), lambda j, i: (l, 0, j)),
        ],
        out_specs=pl.BlockSpec((TM, tn), lambda j, i: (i, j)),
        compiler_params=_params("parallel", "parallel"),
        name="inproj",
    )(x, g, w)


def _mix_prompt_kernel(tiles_per_seq, widths, p_ref, pw_ref, ps_ref, cw_ref, sn_ref, sw_ref,
                       sb_ref, o_ref, zt_ref, zzp_ref, zzc_ref):
    pool_w, conv_w, sgu_w = widths
    o1 = pool_w
    o2 = o1 + conv_w
    o3 = o2 + conv_w
    o4 = o3 + conv_w
    o5 = o4 + sgu_w
    ph = 16
    ch = 8
    tile_in_seq = pl.program_id(0) % tiles_per_seq

    @pl.when(tile_in_seq == 0)
    def _():
        zzp_ref[0:ph, :] = jnp.zeros((ph, pool_w), F32)
        zzc_ref[0:ch, :] = jnp.zeros((ch, conv_w), F32)

    @pl.when(tile_in_seq != 0)
    def _():
        zzp_ref[0:ph, :] = zzp_ref[TM:TM + ph, :]
        zzc_ref[0:ch, :] = zzc_ref[TM:TM + ch, :]

    zzp_ref[ph:ph + TM, :] = p_ref[:, 0:o1]
    pos = tile_in_seq * TM + lax.broadcasted_iota(jnp.int32, (TM, 1), 0)
    for g, w in enumerate(POOL_WINDOWS):
        c0, c1 = g * HEAD, (g + 1) * HEAD
        xa = p_ref[:, c0:c1]
        s = xa
        for j in range(1, w):
            s = s + zzp_ref[ph - j:ph - j + TM, c0:c1]
        cnt = jnp.minimum(w, pos + 1).astype(F32)
        pooled = s / cnt - xa
        y = jnp.dot(pooled.astype(BF16), pw_ref[g], preferred_element_type=F32)
        o_ref[:, c0:c1] = (y * ps_ref[:, c0:c1]).astype(BF16)

    z = p_ref[:, o3:o4] * p_ref[:, o1:o2]
    zzc_ref[ch:ch + TM, :] = z
    y = (cw_ref[2:3, :] * z + cw_ref[1:2, :] * zzc_ref[ch - 1:ch - 1 + TM, :]
         + cw_ref[0:1, :] * zzc_ref[ch - 2:ch - 2 + TM, :])
    o_ref[:, o1:o2] = (p_ref[:, o2:o3] * y).astype(BF16)
    zt_ref[...] = zzc_ref[TM:TM + ch, :]

    vn = _rmsnorm(p_ref[:, o5:o5 + sgu_w], sn_ref[...]).astype(BF16)
    row = lax.broadcasted_iota(jnp.int32, (HEAD, HEAD), 0)
    col = lax.broadcasted_iota(jnp.int32, (HEAD, HEAD), 1)
    for h in range(sgu_w // HEAD):
        wh = jnp.where(row >= col, sw_ref[h], jnp.zeros((HEAD, HEAD), BF16))
        h0, h1 = h * HEAD, (h + 1) * HEAD
        for c in range(TM // HEAD):
            r0, r1 = c * HEAD, (c + 1) * HEAD
            gate = jnp.dot(wh, vn[r0:r1, h0:h1], preferred_element_type=F32) + sb_ref[:, h0:h1]
            o_ref[r0:r1, o2 + h0:o2 + h1] = (p_ref[r0:r1, o4 + h0:o4 + h1] * gate).astype(BF16)


def _mix_prompt(l, p, seq, widths, pool_w, pool_scale, conv_w, sgu_norm, sgu_w, sgu_bx):
    pw, cw, sw = widths
    n_rows = p.shape[0]
    n_tiles = n_rows // TM
    params = (pool_w, pool_scale, conv_w, sgu_norm, sgu_w, sgu_bx)
    return pl.pallas_call(
        functools.partial(_mix_prompt_kernel, seq // TM, widths),
        out_shape=(jax.ShapeDtypeStruct((n_rows, pw + cw + sw), BF16),
                   jax.ShapeDtypeStruct((n_tiles * 8, cw), F32)),
        grid=(n_tiles,),
        in_specs=[pl.BlockSpec((TM, p.shape[1]), lambda i: (i, 0))] + [_layer_spec(a, l, 1) for a in params],
        out_specs=(pl.BlockSpec((TM, pw + cw + sw), lambda i: (i, 0)),
                   pl.BlockSpec((8, cw), lambda i: (i, 0))),
        scratch_shapes=[pltpu.VMEM((16 + TM, pw), F32), pltpu.VMEM((8 + TM, cw), F32)],
        compiler_params=_params("arbitrary"),
        name="mix_prompt",
    )(p, *params)


def _mix_sample_kernel(dec_seq, widths, p_ref, sp_ref, sc_ref, pw_ref, ps_ref, cw_ref, sn_ref,
                       wx_ref, sb_ref, o_ref, zs_ref, vn_ref):
    pool_w, conv_w, sgu_w = widths
    o1 = pool_w
    o2 = o1 + conv_w
    o3 = o2 + conv_w
    o4 = o3 + conv_w
    o5 = o4 + sgu_w
    sb = TM // dec_seq
    slab = lambda t: slice(t * sb, (t + 1) * sb)

    def zz(k, c0, c1):
        if k < POOL_HIST:
            return sp_ref[slab(k), c0:c1]
        return p_ref[slab(k - POOL_HIST), c0:c1]

    for g, w in enumerate(POOL_WINDOWS):
        c0, c1 = g * HEAD, (g + 1) * HEAD
        for t in range(dec_seq):
            s = zz(POOL_HIST + t, c0, c1)
            for j in range(1, w):
                s = s + zz(POOL_HIST + t - j, c0, c1)
            cnt = float(min(w, PAST_LEN + t + 1))
            pooled = s / cnt - p_ref[slab(t), c0:c1]
            y = jnp.dot(pooled.astype(BF16), pw_ref[g], preferred_element_type=F32)
            o_ref[slab(t), c0:c1] = (y * ps_ref[:, c0:c1]).astype(BF16)

    def zc(k):
        if k < SHORT_CONV - 1:
            return sc_ref[slab(k), :]
        t = k - (SHORT_CONV - 1)
        return p_ref[slab(t), o3:o4] * p_ref[slab(t), o1:o2]

    for t in range(dec_seq):
        y = cw_ref[0:1, :] * zc(t) + cw_ref[1:2, :] * zc(t + 1) + cw_ref[2:3, :] * zc(t + 2)
        o_ref[slab(t), o1:o2] = (p_ref[slab(t), o2:o3] * y).astype(BF16)
    for k in range(SHORT_CONV - 1):
        zs_ref[slab(k), :] = zc(dec_seq + k)

    vn_ref[...] = _rmsnorm(p_ref[:, o5:o5 + sgu_w], sn_ref[...])
    for t in range(dec_seq):
        gate = sb_ref[t:t + 1, :] + wx_ref[t * dec_seq:t * dec_seq + 1, :] * vn_ref[slab(0), :]
        for s in range(1, t + 1):
            gate = gate + wx_ref[t * dec_seq + s:t * dec_seq + s + 1, :] * vn_ref[slab(s), :]
        o_ref[slab(t), o2:o2 + sgu_w] = (p_ref[slab(t), o4:o5] * gate).astype(BF16)


def _mix_sample(l, p, dec_seq, widths, st_pool, st_conv, pool_w, pool_scale, conv_w, sgu_norm, sgu_wx,
                sgu_bx):
    pw, cw, sw = widths
    n_rows = p.shape[0]
    n_tiles = n_rows // TM
    sb = TM // dec_seq
    params = (pool_w, pool_scale, conv_w, sgu_norm, sgu_wx, sgu_bx)
    return pl.pallas_call(
        functools.partial(_mix_sample_kernel, dec_seq, widths),
        out_shape=(jax.ShapeDtypeStruct((n_rows, pw + cw + sw), BF16),
                   jax.ShapeDtypeStruct((n_tiles * (SHORT_CONV - 1) * sb, cw), F32),
                   jax.ShapeDtypeStruct((n_rows, sw), F32)),
        grid=(n_tiles,),
        in_specs=[pl.BlockSpec((TM, p.shape[1]), lambda i: (i, 0)),
                  pl.BlockSpec((None, POOL_HIST * sb, pw), lambda i: (l, i, 0)),
                  pl.BlockSpec((None, (SHORT_CONV - 1) * sb, cw), lambda i: (l, i, 0))]
        + [_layer_spec(a, l, 1) for a in params],
        out_specs=(pl.BlockSpec((TM, pw + cw + sw), lambda i: (i, 0)),
                   pl.BlockSpec(((SHORT_CONV - 1) * sb, cw), lambda i: (i, 0)),
                   pl.BlockSpec((TM, sw), lambda i: (i, 0))),
        compiler_params=_params("parallel"),
        name="mix_sample",
    )(p, st_pool, st_conv, *params)


def _outproj_kernel(mix_ref, w_ref, x_ref, g_ref, o_ref, wb_ref):
    @pl.when(pl.program_id(0) == 0)
    def _():
        wb_ref[...] = w_ref[...].astype(BF16)

    a = jnp.dot(mix_ref[...], wb_ref[...], preferred_element_type=F32)
    o_ref[...] = x_ref[...] + _rmsnorm(a, g_ref[...])


def _outproj(l, mix, w, x, g):
    m, d = x.shape
    k = mix.shape[1]
    return pl.pallas_call(
        _outproj_kernel,
        out_shape=jax.ShapeDtypeStruct((m, d), F32),
        grid=(m // TM,),
        in_specs=[pl.BlockSpec((TM, k), lambda i: (i, 0)),
                  pl.BlockSpec((None, k, d), lambda i: (l, 0, 0), pipeline_mode=pl.Buffered(1)),
                  pl.BlockSpec((TM, d), lambda i: (i, 0)),
                  _layer_spec(g, l, 1)],
        out_specs=pl.BlockSpec((TM, d), lambda i: (i, 0)),
        scratch_shapes=[pltpu.VMEM((k, d), BF16)],
        compiler_params=_params("arbitrary"),
        name="outproj",
    )(mix, w, x, g)


def _ffn_kernel(shift, tiles_per_seq, has_prev, *refs):
    if has_prev:
        (x_ref, gpre_ref, upg_ref, upv_ref, cw_ref, cb_ref, dn_ref, gpost_ref, prev_ref,
         o_ref, gt_ref, hn_ref, acc_ref, gbuf_ref) = refs
    else:
        (x_ref, gpre_ref, upg_ref, upv_ref, cw_ref, cb_ref, dn_ref, gpost_ref,
         o_ref, gt_ref, hn_ref, acc_ref, gbuf_ref, carry_ref) = refs
    f = pl.program_id(1)
    nf = pl.num_programs(1)
    hist = gbuf_ref.shape[0] - TM
    keep = gt_ref.shape[0]

    @pl.when(f == 0)
    def _():
        hn_ref[...] = _rmsnorm(x_ref[...], gpre_ref[...]).astype(BF16)
        acc_ref[...] = jnp.zeros(acc_ref.shape, F32)

    if has_prev:
        gbuf_ref[0:hist, :] = prev_ref[...]
    else:
        tile_in_seq = pl.program_id(0) % tiles_per_seq

        @pl.when(tile_in_seq == 0)
        def _():
            gbuf_ref[0:hist, :] = jnp.zeros((hist, TF), F32)

        @pl.when(tile_in_seq != 0)
        def _():
            gbuf_ref[0:hist, :] = carry_ref[f]

    cols = TF // FFN_SPLIT
    total = None
    for c in range(FFN_SPLIT):
        cs = slice(c * cols, (c + 1) * cols)
        gate = jnp.dot(hn_ref[...], upg_ref[:, cs], preferred_element_type=F32)
        val = jnp.dot(hn_ref[...], upv_ref[:, cs], preferred_element_type=F32)
        gbuf_ref[hist:hist + TM, cs] = gate
        gc = (cw_ref[2:3, cs] * gate
              + cw_ref[1:2, cs] * gbuf_ref[hist - shift:hist - shift + TM, cs]
              + cw_ref[0:1, cs] * gbuf_ref[hist - 2 * shift:hist - 2 * shift + TM, cs]
              + cb_ref[:, cs])
        h = (_gelu_tanh(gc) * val).astype(BF16)
        part = jnp.dot(h, dn_ref[cs, :].astype(BF16), preferred_element_type=F32)
        total = part if total is None else total + part
    acc_ref[...] += total

    tail = gbuf_ref[hist + TM - keep:hist + TM, :]
    gt_ref[...] = tail
    if not has_prev:
        carry_ref[f] = tail

    @pl.when(f == nf - 1)
    def _():
        o_ref[...] = x_ref[...] + _rmsnorm(acc_ref[...], gpost_ref[...])


def _ffn(l, x, g_pre, up, conv_w, conv_b, down, g_post, *, seq=None, dec_seq=None, prev=None):
    m, d = x.shape
    dff = down.shape[1]
    nf = dff // TF
    n_tiles = m // TM
    sample = prev is not None
    if sample:
        shift = TM // dec_seq
        hist = keep = (FFN_CONV - 1) * shift
        tiles_per_seq = None
    else:
        shift = 1
        hist = keep = 8
        tiles_per_seq = seq // TM
    in_specs = [pl.BlockSpec((TM, d), lambda i, j: (i, 0)),
                _layer_spec(g_pre, l, 2),
                pl.BlockSpec((None, d, TF), lambda i, j: (l, 0, j)),
                pl.BlockSpec((None, d, TF), lambda i, j: (l, 0, nf + j)),
                pl.BlockSpec((None, FFN_CONV, TF), lambda i, j: (l, 0, j)),
                pl.BlockSpec((None, 1, TF), lambda i, j: (l, 0, j)),
                pl.BlockSpec((None, TF, d), lambda i, j: (l, j, 0)),
                _layer_spec(g_post, l, 2)]
    args = [x, g_pre, up, up, conv_w, conv_b, down, g_post]
    scratch = [pltpu.VMEM((TM, d), BF16), pltpu.VMEM((TM, d), F32), pltpu.VMEM((hist + TM, TF), F32)]
    if sample:
        in_specs.append(pl.BlockSpec((None, hist, TF), lambda i, j: (l, i, j)))
        args.append(prev)
    else:
        scratch.append(pltpu.VMEM((nf, keep, TF), F32))
    return pl.pallas_call(
        functools.partial(_ffn_kernel, shift, tiles_per_seq, sample),
        out_shape=(jax.ShapeDtypeStruct((m, d), F32),
                   jax.ShapeDtypeStruct((n_tiles * keep, dff), F32)),
        grid=(n_tiles, nf),
        in_specs=in_specs,
        out_specs=(pl.BlockSpec((TM, d), lambda i, j: (i, 0)),
                   pl.BlockSpec((keep, TF), lambda i, j: (i, j))),
        scratch_shapes=scratch,
        compiler_params=_params("arbitrary", "arbitrary"),
        name="ffn_sample" if sample else "ffn_prompt",
    )(*args)


def _to_tiles(a, sb):
    *lead, b, t, c = a.shape
    n = len(lead)
    a = a.reshape(*lead, b // sb, sb, t, c)
    a = jnp.swapaxes(a, n + 1, n + 2)
    return a.reshape(*lead, b * t, c)


def _from_tiles(a, t, sb):
    c = a.shape[-1]
    n = a.shape[0] // (t * sb)
    return a.reshape(n, t, sb, c).transpose(0, 2, 1, 3).reshape(n * sb, t, c)


def kernel(x_prompt, x_sample, state_pool, state_conv, state_ffn, g_mix_pre, g_mix_post, g_ffn_pre,
           g_ffn_post, w_in, pool_w, pool_scale, conv_w, sgu_norm, sgu_w, sgu_b, w_out, ffn_up,
           ffn_conv_w, ffn_conv_b, ffn_down):
    batch, seq, d = x_prompt.shape
    dec_batch, dec_seq, _ = x_sample.shape
    depth = w_in.shape[0]
    pw = pool_scale.shape[1]
    cw = conv_w.shape[2]
    sw = sgu_norm.shape[1]
    widths = (pw, cw, sw)
    assert seq % TM == 0 and TM % dec_seq == 0 and TM % HEAD == 0
    assert PAST_LEN % HEAD == 0 and dec_seq <= HEAD
    sb = TM // dec_seq
    assert dec_batch % sb == 0
    tps = seq // TM

    xp = x_prompt.reshape(batch * seq, d)
    xs = _to_tiles(x_sample, sb)
    st_pool = _to_tiles(state_pool, sb)
    st_conv = _to_tiles(state_conv, sb)
    st_ffn = _to_tiles(state_ffn, sb)

    vec = lambda a: a[:, None, :]
    g_mix_pre, g_mix_post, g_ffn_pre, g_ffn_post = map(vec, (g_mix_pre, g_mix_post, g_ffn_pre, g_ffn_post))
    pool_scale, sgu_norm, ffn_conv_b = map(vec, (pool_scale, sgu_norm, ffn_conv_b))
    w_in_b = w_in.astype(BF16)
    up_b = ffn_up.astype(BF16)
    pool_w_b = pool_w.astype(BF16)
    sgu_w_b = sgu_w.astype(BF16)
    sgu_bx = jnp.repeat(jnp.swapaxes(sgu_b, 1, 2), HEAD, axis=2)
    sgu_wx = jnp.repeat(jnp.transpose(sgu_w[:, :, :dec_seq, :dec_seq], (0, 2, 3, 1)), HEAD, axis=3)
    sgu_wx = sgu_wx.reshape(depth, dec_seq * dec_seq, sw)

    outs = {k: [] for k in ("pool_p", "pool_s", "conv_p", "conv_s", "ffn_p", "ffn_s", "v_s")}
    for l in range(depth):
        pp = _inproj(l, xp, g_mix_pre, w_in_b)
        ps = _inproj(l, xs, g_mix_pre, w_in_b)
        mix_p, zt = _mix_prompt(l, pp, seq, widths, pool_w_b, pool_scale, conv_w, sgu_norm, sgu_w_b, sgu_bx)
        mix_s, zs, vn = _mix_sample(l, ps, dec_seq, widths, st_pool, st_conv, pool_w_b, pool_scale, conv_w,
                                    sgu_norm, sgu_wx, sgu_bx)
        xp = _outproj(l, mix_p, w_out, xp, g_mix_post)
        xs = _outproj(l, mix_s, w_out, xs, g_mix_post)
        xp, gt_p = _ffn(l, xp, g_ffn_pre, up_b, ffn_conv_w, ffn_conv_b, ffn_down, g_ffn_post, seq=seq)
        xs, gt_s = _ffn(l, xs, g_ffn_pre, up_b, ffn_conv_w, ffn_conv_b, ffn_down, g_ffn_post,
                        dec_seq=dec_seq, prev=st_ffn)

        xa_s = _from_tiles(ps[:, :pw], dec_seq, sb)
        outs["pool_p"].append(pp.reshape(batch, seq, -1)[:, seq - POOL_HIST:, :pw])
        outs["pool_s"].append(jnp.concatenate([state_pool[l], xa_s], axis=1)[:, -POOL_HIST:])
        outs["conv_p"].append(zt.reshape(batch, tps, 8, cw)[:, -1, 8 - (SHORT_CONV - 1):])
        outs["conv_s"].append(_from_tiles(zs, SHORT_CONV - 1, sb))
        outs["ffn_p"].append(gt_p.reshape(batch, tps, 8, -1)[:, -1, 8 - (FFN_CONV - 1):])
        outs["ffn_s"].append(_from_tiles(gt_s, FFN_CONV - 1, sb))
        outs["v_s"].append(_from_tiles(vn, dec_seq, sb))

    st = lambda k: jnp.stack(outs[k])
    return (xp.reshape(batch, seq, d), _from_tiles(xs, dec_seq, sb), st("pool_p"), st("pool_s"),
            st("conv_p"), st("conv_s"), st("ffn_p"), st("ffn_s"), st("v_s"))
```

```python
import functools

import jax
import jax.numpy as jnp
from jax import lax
from jax.experimental import pallas as pl
from jax.experimental.pallas import tpu as pltpu

F32 = jnp.float32
BF16 = jnp.bfloat16

EPS = 1e-6
PAST_LEN = 16384
POOL_WINDOWS = (2, 4, 8, 16)
POOL_HIST = max(POOL_WINDOWS) - 1
HEAD = 128
SHORT_CONV = 3
FFN_CONV = 3

TM = 512
TF = 512
FFN_SPLIT = 2
VMEM_LIMIT = 56 * 1024 * 1024


def _rmsnorm(x, g):
    ms = jnp.mean(x * x, axis=-1, keepdims=True)
    return x * lax.rsqrt(ms + EPS) * g


def _gelu_tanh(x):
    c = 0.7978845608028654
    return x * (0.5 * (1.0 + jnp.tanh(c * (x + 0.044715 * (x * x * x)))))


def _params(*sem):
    return pltpu.CompilerParams(dimension_semantics=sem, vmem_limit_bytes=VMEM_LIMIT)


def _layer_spec(a, l, n_grid):
    zeros = (0,) * (a.ndim - 1)
    if n_grid == 1:
        return pl.BlockSpec((None,) + a.shape[1:], lambda i: (l,) + zeros)
    return pl.BlockSpec((None,) + a.shape[1:], lambda i, j: (l,) + zeros)


def _inproj_kernel(x_ref, g_ref, w_ref, o_ref):
    h = _rmsnorm(x_ref[...], g_ref[...]).astype(BF16)
    o_ref[...] = jnp.dot(h, w_ref[...], preferred_element_type=F32)


def _inproj(l, x, g, w):
    m, d = x.shape
    n = w.shape[2]
    tn = n // 2
    return pl.pallas_call(
        _inproj_kernel,
        out_shape=jax.ShapeDtypeStruct((m, n), F32),
        grid=(n // tn, m // TM),
        in_specs=[
            pl.BlockSpec((TM, d), lambda j, i: (i, 0)),
            _layer_spec(g, l, 2),
            pl.BlockSpec((None, d, tn), lambda j, i: (l, 0, j)),
        ],
        out_specs=pl.BlockSpec((TM, tn), lambda j, i: (i, j)),
        compiler_params=_params("parallel", "parallel"),
        name="inproj",
    )(x, g, w)


def _mix_prompt_kernel(tiles_per_seq, widths, p_ref, pw_ref, ps_ref, cw_ref, sn_ref, sw_ref,
                       sb_ref, o_ref, zt_ref, zzp_ref, zzc_ref):
    pool_w, conv_w, sgu_w = widths
    o1 = pool_w
    o2 = o1 + conv_w
    o3 = o2 + conv_w
    o4 = o3 + conv_w
    o5 = o4 + sgu_w
    ph = 16
    ch = 8
    tile_in_seq = pl.program_id(0) % tiles_per_seq

    @pl.when(tile_in_seq == 0)
    def _():
        zzp_ref[0:ph, :] = jnp.zeros((ph, pool_w), F32)
        zzc_ref[0:ch, :] = jnp.zeros((ch, conv_w), F32)

    @pl.when(tile_in_seq != 0)
    def _():
        zzp_ref[0:ph, :] = zzp_ref[TM:TM + ph, :]
        zzc_ref[0:ch, :] = zzc_ref[TM:TM + ch, :]

    zzp_ref[ph:ph + TM, :] = p_ref[:, 0:o1]
    pos = tile_in_seq * TM + lax.broadcasted_iota(jnp.int32, (TM, 1), 0)
    for g, w in enumerate(POOL_WINDOWS):
        c0, c1 = g * HEAD, (g + 1) * HEAD
        xa = p_ref[:, c0:c1]
        s = xa
        for j in range(1, w):
            s = s + zzp_ref[ph - j:ph - j + TM, c0:c1]
        cnt = jnp.minimum(w, pos + 1).astype(F32)
        pooled = s / cnt - xa
        y = jnp.dot(pooled.astype(BF16), pw_ref[g], preferred_element_type=F32)
        o_ref[:, c0:c1] = (y * ps_ref[:, c0:c1]).astype(BF16)

    z = p_ref[:, o3:o4] * p_ref[:, o1:o2]
    zzc_ref[ch:ch + TM, :] = z
    y = (cw_ref[2:3, :] * z + cw_ref[1:2, :] * zzc_ref[ch - 1:ch - 1 + TM, :]
         + cw_ref[0:1, :] * zzc_ref[ch - 2:ch - 2 + TM, :])
    o_ref[:, o1:o2] = (p_ref[:, o2:o3] * y).astype(BF16)
    zt_ref[...] = zzc_ref[TM:TM + ch, :]

    vn = _rmsnorm(p_ref[:, o5:o5 + sgu_w], sn_ref[...]).astype(BF16)
    row = lax.broadcasted_iota(jnp.int32, (HEAD, HEAD), 0)
    col = lax.broadcasted_iota(jnp.int32, (HEAD, HEAD), 1)
    for h in range(sgu_w // HEAD):
        wh = jnp.where(row >= col, sw_ref[h], jnp.zeros((HEAD, HEAD), BF16))
        h0, h1 = h * HEAD, (h + 1) * HEAD
        for c in range(TM // HEAD):
            r0, r1 = c * HEAD, (c + 1) * HEAD
            gate = jnp.dot(wh, vn[r0:r1, h0:h1], preferred_element_type=F32) + sb_ref[:, h0:h1]
            o_ref[r0:r1, o2 + h0:o2 + h1] = (p_ref[r0:r1, o4 + h0:o4 + h1] * gate).astype(BF16)


def _mix_prompt(l, p, seq, widths, pool_w, pool_scale, conv_w, sgu_norm, sgu_w, sgu_bx):
    pw, cw, sw = widths
    n_rows = p.shape[0]
    n_tiles = n_rows // TM
    params = (pool_w, pool_scale, conv_w, sgu_norm, sgu_w, sgu_bx)
    return pl.pallas_call(
        functools.partial(_mix_prompt_kernel, seq // TM, widths),
        out_shape=(jax.ShapeDtypeStruct((n_rows, pw + cw + sw), BF16),
                   jax.ShapeDtypeStruct((n_tiles * 8, cw), F32)),
        grid=(n_tiles,),
        in_specs=[pl.BlockSpec((TM, p.shape[1]), lambda i: (i, 0))] + [_layer_spec(a, l, 1) for a in params],
        out_specs=(pl.BlockSpec((TM, pw + cw + sw), lambda i: (i, 0)),
                   pl.BlockSpec((8, cw), lambda i: (i, 0))),
        scratch_shapes=[pltpu.VMEM((16 + TM, pw), F32), pltpu.VMEM((8 + TM, cw), F32)],
        compiler_params=_params("arbitrary"),
        name="mix_prompt",
    )(p, *params)


def _mix_sample_kernel(dec_seq, widths, p_ref, sp_ref, sc_ref, pw_ref, ps_ref, cw_ref, sn_ref,
                       wx_ref, sb_ref, o_ref, zs_ref, vn_ref):
    pool_w, conv_w, sgu_w = widths
    o1 = pool_w
    o2 = o1 + conv_w
    o3 = o2 + conv_w
    o4 = o3 + conv_w
    o5 = o4 + sgu_w
    sb = TM // dec_seq
    slab = lambda t: slice(t * sb, (t + 1) * sb)

    def zz(k, c0, c1):
        if k < POOL_HIST:
            return sp_ref[slab(k), c0:c1]
        return p_ref[slab(k - POOL_HIST), c0:c1]

    for g, w in enumerate(POOL_WINDOWS):
        c0, c1 = g * HEAD, (g + 1) * HEAD
        for t in range(dec_seq):
            s = zz(POOL_HIST + t, c0, c1)
            for j in range(1, w):
                s = s + zz(POOL_HIST + t - j, c0, c1)
            cnt = float(min(w, PAST_LEN + t + 1))
            pooled = s / cnt - p_ref[slab(t), c0:c1]
            y = jnp.dot(pooled.astype(BF16), pw_ref[g], preferred_element_type=F32)
            o_ref[slab(t), c0:c1] = (y * ps_ref[:, c0:c1]).astype(BF16)

    def zc(k):
        if k < SHORT_CONV - 1:
            return sc_ref[slab(k), :]
        t = k - (SHORT_CONV - 1)
        return p_ref[slab(t), o3:o4] * p_ref[slab(t), o1:o2]

    for t in range(dec_seq):
        y = cw_ref[0:1, :] * zc(t) + cw_ref[1:2, :] * zc(t + 1) + cw_ref[2:3, :] * zc(t + 2)
        o_ref[slab(t), o1:o2] = (p_ref[slab(t), o2:o3] * y).astype(BF16)
    for k in range(SHORT_CONV - 1):
        zs_ref[slab(k), :] = zc(dec_seq + k)

    vn_ref[...] = _rmsnorm(p_ref[:, o5:o5 + sgu_w], sn_ref[...])
    for t in range(dec_seq):
        gate = sb_ref[t:t + 1, :] + wx_ref[t * dec_seq:t * dec_seq + 1, :] * vn_ref[slab(0), :]
        for s in range(1, t + 1):
            gate = gate + wx_ref[t * dec_seq + s:t * dec_seq + s + 1, :] * vn_ref[slab(s), :]
        o_ref[slab(t), o2:o2 + sgu_w] = (p_ref[slab(t), o4:o5] * gate).astype(BF16)


def _mix_sample(l, p, dec_seq, widths, st_pool, st_conv, pool_w, pool_scale, conv_w, sgu_norm, sgu_wx,
                sgu_bx):
    pw, cw, sw = widths
    n_rows = p.shape[0]
    n_tiles = n_rows // TM
    sb = TM // dec_seq
    params = (pool_w, pool_scale, conv_w, sgu_norm, sgu_wx, sgu_bx)
    return pl.pallas_call(
        functools.partial(_mix_sample_kernel, dec_seq, widths),
        out_shape=(jax.ShapeDtypeStruct((n_rows, pw + cw + sw), BF16),
                   jax.ShapeDtypeStruct((n_tiles * (SHORT_CONV - 1) * sb, cw), F32),
                   jax.ShapeDtypeStruct((n_rows, sw), F32)),
        grid=(n_tiles,),
        in_specs=[pl.BlockSpec((TM, p.shape[1]), lambda i: (i, 0)),
                  pl.BlockSpec((None, POOL_HIST * sb, pw), lambda i: (l, i, 0)),
                  pl.BlockSpec((None, (SHORT_CONV - 1) * sb, cw), lambda i: (l, i, 0))]
        + [_layer_spec(a, l, 1) for a in params],
        out_specs=(pl.BlockSpec((TM, pw + cw + sw), lambda i: (i, 0)),
                   pl.BlockSpec(((SHORT_CONV - 1) * sb, cw), lambda i: (i, 0)),
                   pl.BlockSpec((TM, sw), lambda i: (i, 0))),
        compiler_params=_params("parallel"),
        name="mix_sample",
    )(p, st_pool, st_conv, *params)


def _outproj_kernel(mix_ref, w_ref, x_ref, g_ref, o_ref):
    a = jnp.dot(mix_ref[...], w_ref[...], preferred_element_type=F32)
    o_ref[...] = x_ref[...] + _rmsnorm(a, g_ref[...])


def _outproj(l, mix, w, x, g):
    m, d = x.shape
    k = mix.shape[1]
    return pl.pallas_call(
        _outproj_kernel,
        out_shape=jax.ShapeDtypeStruct((m, d), F32),
        grid=(m // TM,),
        in_specs=[pl.BlockSpec((TM, k), lambda i: (i, 0)),
                  _layer_spec(w, l, 1),
                  pl.BlockSpec((TM, d), lambda i: (i, 0)),
                  _layer_spec(g, l, 1)],
        out_specs=pl.BlockSpec((TM, d), lambda i: (i, 0)),
        compiler_params=_params("parallel"),
        name="outproj",
    )(mix, w, x, g)


def _ffn_kernel(shift, tiles_per_seq, has_prev, *refs):
    if has_prev:
        (x_ref, gpre_ref, upg_ref, upv_ref, cw_ref, cb_ref, dn_ref, gpost_ref, prev_ref,
         o_ref, gt_ref, hn_ref, acc_ref, gbuf_ref) = refs
    else:
        (x_ref, gpre_ref, upg_ref, upv_ref, cw_ref, cb_ref, dn_ref, gpost_ref,
         o_ref, gt_ref, hn_ref, acc_ref, gbuf_ref, carry_ref) = refs
    f = pl.program_id(1)
    nf = pl.num_programs(1)
    hist = gbuf_ref.shape[0] - TM
    keep = gt_ref.shape[1]

    @pl.when(f == 0)
    def _():
        hn_ref[...] = _rmsnorm(x_ref[...], gpre_ref[...]).astype(BF16)
        acc_ref[...] = jnp.zeros(acc_ref.shape, F32)

    if has_prev:
        gbuf_ref[0:hist, :] = prev_ref[...]
    else:
        tile_in_seq = pl.program_id(0) % tiles_per_seq

        @pl.when(tile_in_seq == 0)
        def _():
            gbuf_ref[0:hist, :] = jnp.zeros((hist, TF), F32)

        @pl.when(tile_in_seq != 0)
        def _():
            gbuf_ref[0:hist, :] = carry_ref[f]

    cols = TF // FFN_SPLIT
    total = None
    for c in range(FFN_SPLIT):
        cs = slice(c * cols, (c + 1) * cols)
        gate = jnp.dot(hn_ref[...], upg_ref[:, cs], preferred_element_type=F32)
        val = jnp.dot(hn_ref[...], upv_ref[:, cs], preferred_element_type=F32)
        gbuf_ref[hist:hist + TM, cs] = gate
        gc = (cw_ref[f, 2:3, cs] * gate
              + cw_ref[f, 1:2, cs] * gbuf_ref[hist - shift:hist - shift + TM, cs]
              + cw_ref[f, 0:1, cs] * gbuf_ref[hist - 2 * shift:hist - 2 * shift + TM, cs]
              + cb_ref[f, :, cs])
        h = (_gelu_tanh(gc) * val).astype(BF16)
        part = jnp.dot(h, dn_ref[cs, :].astype(BF16), preferred_element_type=F32)
        total = part if total is None else total + part
    acc_ref[...] += total

    tail = gbuf_ref[hist + TM - keep:hist + TM, :]
    gt_ref[f] = tail
    if not has_prev:
        carry_ref[f] = tail

    @pl.when(f == nf - 1)
    def _():
        o_ref[...] = x_ref[...] + _rmsnorm(acc_ref[...], gpost_ref[...])


def _ffn(l, x, g_pre, up, conv_w, conv_b, down, g_post, *, seq=None, dec_seq=None, prev=None):
    m, d = x.shape
    dff = down.shape[1]
    nf = dff // TF
    n_tiles = m // TM
    sample = prev is not None
    if sample:
        shift = TM // dec_seq
        hist = keep = (FFN_CONV - 1) * shift
        tiles_per_seq = None
    else:
        shift = 1
        hist = keep = 8
        tiles_per_seq = seq // TM
    conv_w = jnp.swapaxes(conv_w.reshape(conv_w.shape[0], FFN_CONV, nf, TF), 1, 2)
    conv_b = conv_b.reshape(conv_b.shape[0], nf, 1, TF)
    in_specs = [pl.BlockSpec((TM, d), lambda i, j: (i, 0)),
                _layer_spec(g_pre, l, 2),
                pl.BlockSpec((None, d, TF), lambda i, j: (l, 0, j)),
                pl.BlockSpec((None, d, TF), lambda i, j: (l, 0, nf + j)),
                _layer_spec(conv_w, l, 2),
                _layer_spec(conv_b, l, 2),
                pl.BlockSpec((None, TF, d), lambda i, j: (l, j, 0)),
                _layer_spec(g_post, l, 2)]
    args = [x, g_pre, up, up, conv_w, conv_b, down, g_post]
    scratch = [pltpu.VMEM((TM, d), BF16), pltpu.VMEM((TM, d), F32), pltpu.VMEM((hist + TM, TF), F32)]
    if sample:
        in_specs.append(pl.BlockSpec((None, hist, TF), lambda i, j: (l, i, j)))
        args.append(prev)
    else:
        scratch.append(pltpu.VMEM((nf, keep, TF), F32))
    y, gt = pl.pallas_call(
        functools.partial(_ffn_kernel, shift, tiles_per_seq, sample),
        out_shape=(jax.ShapeDtypeStruct((m, d), F32),
                   jax.ShapeDtypeStruct((n_tiles, nf, keep, TF), F32)),
        grid=(n_tiles, nf),
        in_specs=in_specs,
        out_specs=(pl.BlockSpec((TM, d), lambda i, j: (i, 0)),
                   pl.BlockSpec((None, nf, keep, TF), lambda i, j: (i, 0, 0, 0))),
        scratch_shapes=scratch,
        compiler_params=_params("arbitrary", "arbitrary"),
        name="ffn_sample" if sample else "ffn_prompt",
    )(*args)
    return y, jnp.swapaxes(gt, 1, 2).reshape(n_tiles * keep, dff)


def _to_tiles(a, sb):
    *lead, b, t, c = a.shape
    n = len(lead)
    a = a.reshape(*lead, b // sb, sb, t, c)
    a = jnp.swapaxes(a, n + 1, n + 2)
    return a.reshape(*lead, b * t, c)


def _from_tiles(a, t, sb):
    c = a.shape[-1]
    n = a.shape[0] // (t * sb)
    return a.reshape(n, t, sb, c).transpose(0, 2, 1, 3).reshape(n * sb, t, c)


def kernel(x_prompt, x_sample, state_pool, state_conv, state_ffn, g_mix_pre, g_mix_post, g_ffn_pre,
           g_ffn_post, w_in, pool_w, pool_scale, conv_w, sgu_norm, sgu_w, sgu_b, w_out, ffn_up,
           ffn_conv_w, ffn_conv_b, ffn_down):
    batch, seq, d = x_prompt.shape
    dec_batch, dec_seq, _ = x_sample.shape
    depth = w_in.shape[0]
    pw = pool_scale.shape[1]
    cw = conv_w.shape[2]
    sw = sgu_norm.shape[1]
    widths = (pw, cw, sw)
    assert seq % TM == 0 and TM % dec_seq == 0 and TM % HEAD == 0
    assert PAST_LEN % HEAD == 0 and dec_seq <= HEAD
    sb = TM // dec_seq
    assert dec_batch % sb == 0
    tps = seq // TM

    xp = x_prompt.reshape(batch * seq, d)
    xs = _to_tiles(x_sample, sb)
    st_pool = _to_tiles(state_pool, sb)
    st_conv = _to_tiles(state_conv, sb)
    st_ffn = _to_tiles(state_ffn, sb)

    vec = lambda a: a[:, None, :]
    g_mix_pre, g_mix_post, g_ffn_pre, g_ffn_post = map(vec, (g_mix_pre, g_mix_post, g_ffn_pre, g_ffn_post))
    pool_scale, sgu_norm, ffn_conv_b = map(vec, (pool_scale, sgu_norm, ffn_conv_b))
    w_in_b = w_in.astype(BF16)
    w_out_b = w_out.astype(BF16)
    up_b = ffn_up.astype(BF16)
    pool_w_b = pool_w.astype(BF16)
    sgu_w_b = sgu_w.astype(BF16)
    sgu_bx = jnp.repeat(jnp.swapaxes(sgu_b, 1, 2), HEAD, axis=2)
    sgu_wx = jnp.repeat(jnp.transpose(sgu_w[:, :, :dec_seq, :dec_seq], (0, 2, 3, 1)), HEAD, axis=3)
    sgu_wx = sgu_wx.reshape(depth, dec_seq * dec_seq, sw)

    outs = {k: [] for k in ("pool_p", "pool_s", "conv_p", "conv_s", "ffn_p", "ffn_s", "v_s")}
    for l in range(depth):
        pp = _inproj(l, xp, g_mix_pre, w_in_b)
        ps = _inproj(l, xs, g_mix_pre, w_in_b)
        mix_p, zt = _mix_prompt(l, pp, seq, widths, pool_w_b, pool_scale, conv_w, sgu_norm, sgu_w_b, sgu_bx)
        mix_s, zs, vn = _mix_sample(l, ps, dec_seq, widths, st_pool, st_conv, pool_w_b, pool_scale, conv_w,
                                    sgu_norm, sgu_wx, sgu_bx)
        xp = _outproj(l, mix_p, w_out_b, xp, g_mix_post)
        xs = _outproj(l, mix_s, w_out_b, xs, g_mix_post)
        xp, gt_p = _ffn(l, xp, g_ffn_pre, up_b, ffn_conv_w, ffn_conv_b, ffn_down, g_ffn_post, seq=seq)
        xs, gt_s = _ffn(l, xs, g_ffn_pre, up_b, ffn_conv_w, ffn_conv_b, ffn_down, g_ffn_post,
                        dec_seq=dec_seq, prev=st_ffn)

        xa_s = _from_tiles(ps[:, :pw], dec_seq, sb)
        outs["pool_p"].append(pp.reshape(batch, seq, -1)[:, seq - POOL_HIST:, :pw])
        outs["pool_s"].append(jnp.concatenate([state_pool[l], xa_s], axis=1)[:, -POOL_HIST:])
        outs["conv_p"].append(zt.reshape(batch, tps, 8, cw)[:, -1, 8 - (SHORT_CONV - 1):])
        outs["conv_s"].append(_from_tiles(zs, SHORT_CONV - 1, sb))
        outs["ffn_p"].append(gt_p.reshape(batch, tps, 8, -1)[:, -1, 8 - (FFN_CONV - 1):])
        outs["ffn_s"].append(_from_tiles(gt_s, FFN_CONV - 1, sb))
        outs["v_s"].append(_from_tiles(vn, dec_seq, sb))

    st = lambda k: jnp.stack(outs[k])
    return (xp.reshape(batch, seq, d), _from_tiles(xs, dec_seq, sb), st("pool_p"), st("pool_s"),
            st("conv_p"), st("conv_s"), st("ffn_p"), st("ffn_s"), st("v_s"))
```

```python
import functools

import jax
import jax.numpy as jnp
from jax import lax
from jax.experimental import pallas as pl
from jax.experimental.pallas import tpu as pltpu

F32 = jnp.float32
BF16 = jnp.bfloat16

EPS = 1e-6
PAST_LEN = 16384
POOL_WINDOWS = (2, 4, 8, 16)
POOL_HIST = max(POOL_WINDOWS) - 1
HEAD = 128
SHORT_CONV = 3
FFN_CONV = 3

TM = 512
TF = 512
FFN_SPLIT = 2
VMEM_LIMIT = 56 * 1024 * 1024


def _rmsnorm(x, g):
    ms = jnp.mean(x * x, axis=-1, keepdims=True)
    return x * lax.rsqrt(ms + EPS) * g


def _gelu_tanh(x):
    c = 0.7978845608028654
    return x * (0.5 * (1.0 + jnp.tanh(c * (x + 0.044715 * (x * x * x)))))


def _params(*sem):
    return pltpu.CompilerParams(dimension_semantics=sem, vmem_limit_bytes=VMEM_LIMIT)


def _layer_spec(a, l, n_grid):
    zeros = (0,) * (a.ndim - 1)
    if n_grid == 1:
        return pl.BlockSpec((None,) + a.shape[1:], lambda i: (l,) + zeros)
    return pl.BlockSpec((None,) + a.shape[1:], lambda i, j: (l,) + zeros)


def _inproj_kernel(x_ref, g_ref, w_ref, o_ref):
    h = _rmsnorm(x_ref[...], g_ref[...]).astype(BF16)
    o_ref[...] = jnp.dot(h, w_ref[...], preferred_element_type=F32)


def _inproj(l, x, g, w):
    m, d = x.shape
    n = w.shape[2]
    tn = n
    return pl.pallas_call(
        _inproj_kernel,
        out_shape=jax.ShapeDtypeStruct((m, n), F32),
        grid=(n // tn, m // TM),
        in_specs=[
            pl.BlockSpec((TM, d), lambda j, i: (i, 0)),
            _layer_spec(g, l, 2),
            pl.BlockSpec((None, d, tn), lambda j, i: (l, 0, j), pipeline_mode=pl.Buffered(1)),
        ],
        out_specs=pl.BlockSpec((TM, tn), lambda j, i: (i, j)),
        compiler_params=_params("parallel", "parallel"),
        name="inproj",
    )(x, g, w)


def _mix_prompt_kernel(tiles_per_seq, widths, p_ref, pw_ref, ps_ref, cw_ref, sn_ref, sw_ref,
                       sb_ref, o_ref, zt_ref, zzp_ref, zzc_ref):
    pool_w, conv_w, sgu_w = widths
    o1 = pool_w
    o2 = o1 + conv_w
    o3 = o2 + conv_w
    o4 = o3 + conv_w
    o5 = o4 + sgu_w
    ph = 16
    ch = 8
    tile_in_seq = pl.program_id(0) % tiles_per_seq

    @pl.when(tile_in_seq == 0)
    def _():
        zzp_ref[0:ph, :] = jnp.zeros((ph, pool_w), F32)
        zzc_ref[0:ch, :] = jnp.zeros((ch, conv_w), F32)

    @pl.when(tile_in_seq != 0)
    def _():
        zzp_ref[0:ph, :] = zzp_ref[TM:TM + ph, :]
        zzc_ref[0:ch, :] = zzc_ref[TM:TM + ch, :]

    zzp_ref[ph:ph + TM, :] = p_ref[:, 0:o1]
    pos = tile_in_seq * TM + lax.broadcasted_iota(jnp.int32, (TM, 1), 0)
    for g, w in enumerate(POOL_WINDOWS):
        c0, c1 = g * HEAD, (g + 1) * HEAD
        xa = p_ref[:, c0:c1]
        s = xa
        for j in range(1, w):
            s = s + zzp_ref[ph - j:ph - j + TM, c0:c1]
        cnt = jnp.minimum(w, pos + 1).astype(F32)
        pooled = s / cnt - xa
        y = jnp.dot(pooled.astype(BF16), pw_ref[g], preferred_element_type=F32)
        o_ref[:, c0:c1] = (y * ps_ref[:, c0:c1]).astype(BF16)

    z = p_ref[:, o3:o4] * p_ref[:, o1:o2]
    zzc_ref[ch:ch + TM, :] = z
    y = (cw_ref[2:3, :] * z + cw_ref[1:2, :] * zzc_ref[ch - 1:ch - 1 + TM, :]
         + cw_ref[0:1, :] * zzc_ref[ch - 2:ch - 2 + TM, :])
    o_ref[:, o1:o2] = (p_ref[:, o2:o3] * y).astype(BF16)
    zt_ref[...] = zzc_ref[TM:TM + ch, :]

    vn = _rmsnorm(p_ref[:, o5:o5 + sgu_w], sn_ref[...]).astype(BF16)
    row = lax.broadcasted_iota(jnp.int32, (HEAD, HEAD), 0)
    col = lax.broadcasted_iota(jnp.int32, (HEAD, HEAD), 1)
    for h in range(sgu_w // HEAD):
        wh = jnp.where(row >= col, sw_ref[h], jnp.zeros((HEAD, HEAD), BF16))
        h0, h1 = h * HEAD, (h + 1) * HEAD
        for c in range(TM // HEAD):
            r0, r1 = c * HEAD, (c + 1) * HEAD
            gate = jnp.dot(wh, vn[r0:r1, h0:h1], preferred_element_type=F32) + sb_ref[:, h0:h1]
            o_ref[r0:r1, o2 + h0:o2 + h1] = (p_ref[r0:r1, o4 + h0:o4 + h1] * gate).astype(BF16)


def _mix_prompt(l, p, seq, widths, pool_w, pool_scale, conv_w, sgu_norm, sgu_w, sgu_bx):
    pw, cw, sw = widths
    n_rows = p.shape[0]
    n_tiles = n_rows // TM
    params = (pool_w, pool_scale, conv_w, sgu_norm, sgu_w, sgu_bx)
    return pl.pallas_call(
        functools.partial(_mix_prompt_kernel, seq // TM, widths),
        out_shape=(jax.ShapeDtypeStruct((n_rows, pw + cw + sw), BF16),
                   jax.ShapeDtypeStruct((n_tiles * 8, cw), F32)),
        grid=(n_tiles,),
        in_specs=[pl.BlockSpec((TM, p.shape[1]), lambda i: (i, 0))] + [_layer_spec(a, l, 1) for a in params],
        out_specs=(pl.BlockSpec((TM, pw + cw + sw), lambda i: (i, 0)),
                   pl.BlockSpec((8, cw), lambda i: (i, 0))),
        scratch_shapes=[pltpu.VMEM((16 + TM, pw), F32), pltpu.VMEM((8 + TM, cw), F32)],
        compiler_params=_params("arbitrary"),
        name="mix_prompt",
    )(p, *params)


def _mix_sample_kernel(dec_seq, widths, p_ref, sp_ref, sc_ref, pw_ref, ps_ref, cw_ref, sn_ref,
                       wx_ref, sb_ref, o_ref, zs_ref, vn_ref):
    pool_w, conv_w, sgu_w = widths
    o1 = pool_w
    o2 = o1 + conv_w
    o3 = o2 + conv_w
    o4 = o3 + conv_w
    o5 = o4 + sgu_w
    sb = TM // dec_seq
    slab = lambda t: slice(t * sb, (t + 1) * sb)

    def zz(k, c0, c1):
        if k < POOL_HIST:
            return sp_ref[slab(k), c0:c1]
        return p_ref[slab(k - POOL_HIST), c0:c1]

    for g, w in enumerate(POOL_WINDOWS):
        c0, c1 = g * HEAD, (g + 1) * HEAD
        for t in range(dec_seq):
            s = zz(POOL_HIST + t, c0, c1)
            for j in range(1, w):
                s = s + zz(POOL_HIST + t - j, c0, c1)
            cnt = float(min(w, PAST_LEN + t + 1))
            pooled = s / cnt - p_ref[slab(t), c0:c1]
            y = jnp.dot(pooled.astype(BF16), pw_ref[g], preferred_element_type=F32)
            o_ref[slab(t), c0:c1] = (y * ps_ref[:, c0:c1]).astype(BF16)

    def zc(k):
        if k < SHORT_CONV - 1:
            return sc_ref[slab(k), :]
        t = k - (SHORT_CONV - 1)
        return p_ref[slab(t), o3:o4] * p_ref[slab(t), o1:o2]

    for t in range(dec_seq):
        y = cw_ref[0:1, :] * zc(t) + cw_ref[1:2, :] * zc(t + 1) + cw_ref[2:3, :] * zc(t + 2)
        o_ref[slab(t), o1:o2] = (p_ref[slab(t), o2:o3] * y).astype(BF16)
    for k in range(SHORT_CONV - 1):
        zs_ref[slab(k), :] = zc(dec_seq + k)

    vn_ref[...] = _rmsnorm(p_ref[:, o5:o5 + sgu_w], sn_ref[...])
    for t in range(dec_seq):
        gate = sb_ref[t:t + 1, :] + wx_ref[t * dec_seq:t * dec_seq + 1, :] * vn_ref[slab(0), :]
        for s in range(1, t + 1):
            gate = gate + wx_ref[t * dec_seq + s:t * dec_seq + s + 1, :] * vn_ref[slab(s), :]
        o_ref[slab(t), o2:o2 + sgu_w] = (p_ref[slab(t), o4:o5] * gate).astype(BF16)


def _mix_sample(l, p, dec_seq, widths, st_pool, st_conv, pool_w, pool_scale, conv_w, sgu_norm, sgu_wx,
                sgu_bx):
    pw, cw, sw = widths
    n_rows = p.shape[0]
    n_tiles = n_rows // TM
    sb = TM // dec_seq
    params = (pool_w, pool_scale, conv_w, sgu_norm, sgu_wx, sgu_bx)
    return pl.pallas_call(
        functools.partial(_mix_sample_kernel, dec_seq, widths),
        out_shape=(jax.ShapeDtypeStruct((n_rows, pw + cw + sw), BF16),
                   jax.ShapeDtypeStruct((n_tiles * (SHORT_CONV - 1) * sb, cw), F32),
                   jax.ShapeDtypeStruct((n_rows, sw), F32)),
        grid=(n_tiles,),
        in_specs=[pl.BlockSpec((TM, p.shape[1]), lambda i: (i, 0)),
                  pl.BlockSpec((None, POOL_HIST * sb, pw), lambda i: (l, i, 0)),
                  pl.BlockSpec((None, (SHORT_CONV - 1) * sb, cw), lambda i: (l, i, 0))]
        + [_layer_spec(a, l, 1) for a in params],
        out_specs=(pl.BlockSpec((TM, pw + cw + sw), lambda i: (i, 0)),
                   pl.BlockSpec(((SHORT_CONV - 1) * sb, cw), lambda i: (i, 0)),
                   pl.BlockSpec((TM, sw), lambda i: (i, 0))),
        compiler_params=_params("parallel"),
        name="mix_sample",
    )(p, st_pool, st_conv, *params)


def _outproj_kernel(mix_ref, w_ref, x_ref, g_ref, o_ref):
    a = jnp.dot(mix_ref[...], w_ref[...], preferred_element_type=F32)
    o_ref[...] = x_ref[...] + _rmsnorm(a, g_ref[...])


def _outproj(l, mix, w, x, g):
    m, d = x.shape
    k = mix.shape[1]
    return pl.pallas_call(
        _outproj_kernel,
        out_shape=jax.ShapeDtypeStruct((m, d), F32),
        grid=(m // TM,),
        in_specs=[pl.BlockSpec((TM, k), lambda i: (i, 0)),
                  _layer_spec(w, l, 1),
                  pl.BlockSpec((TM, d), lambda i: (i, 0)),
                  _layer_spec(g, l, 1)],
        out_specs=pl.BlockSpec((TM, d), lambda i: (i, 0)),
        compiler_params=_params("parallel"),
        name="outproj",
    )(mix, w, x, g)


def _ffn_kernel(shift, tiles_per_seq, has_prev, *refs):
    if has_prev:
        (x_ref, gpre_ref, upg_ref, upv_ref, cw_ref, cb_ref, dn_ref, gpost_ref, prev_ref,
         o_ref, gt_ref, hn_ref, acc_ref, gbuf_ref) = refs
    else:
        (x_ref, gpre_ref, upg_ref, upv_ref, cw_ref, cb_ref, dn_ref, gpost_ref,
         o_ref, gt_ref, hn_ref, acc_ref, gbuf_ref, carry_ref) = refs
    f = pl.program_id(1)
    nf = pl.num_programs(1)
    hist = gbuf_ref.shape[0] - TM
    keep = gt_ref.shape[0]

    @pl.when(f == 0)
    def _():
        hn_ref[...] = _rmsnorm(x_ref[...], gpre_ref[...]).astype(BF16)
        acc_ref[...] = jnp.zeros(acc_ref.shape, F32)

    if has_prev:
        gbuf_ref[0:hist, :] = prev_ref[...]
    else:
        tile_in_seq = pl.program_id(0) % tiles_per_seq

        @pl.when(tile_in_seq == 0)
        def _():
            gbuf_ref[0:hist, :] = jnp.zeros((hist, TF), F32)

        @pl.when(tile_in_seq != 0)
        def _():
            gbuf_ref[0:hist, :] = carry_ref[f]

    cols = TF // FFN_SPLIT
    total = None
    for c in range(FFN_SPLIT):
        cs = slice(c * cols, (c + 1) * cols)
        gate = jnp.dot(hn_ref[...], upg_ref[:, cs], preferred_element_type=F32)
        val = jnp.dot(hn_ref[...], upv_ref[:, cs], preferred_element_type=F32)
        gbuf_ref[hist:hist + TM, cs] = gate
        gc = (cw_ref[2:3, cs] * gate
              + cw_ref[1:2, cs] * gbuf_ref[hist - shift:hist - shift + TM, cs]
              + cw_ref[0:1, cs] * gbuf_ref[hist - 2 * shift:hist - 2 * shift + TM, cs]
              + cb_ref[:, cs])
        h = (_gelu_tanh(gc) * val).astype(BF16)
        part = jnp.dot(h, dn_ref[cs, :].astype(BF16), preferred_element_type=F32)
        total = part if total is None else total + part
    acc_ref[...] += total

    tail = gbuf_ref[hist + TM - keep:hist + TM, :]
    gt_ref[...] = tail
    if not has_prev:
        carry_ref[f] = tail

    @pl.when(f == nf - 1)
    def _():
        o_ref[...] = x_ref[...] + _rmsnorm(acc_ref[...], gpost_ref[...])


def _ffn(l, x, g_pre, up, conv_w, conv_b, down, g_post, *, seq=None, dec_seq=None, prev=None):
    m, d = x.shape
    dff = down.shape[1]
    nf = dff // TF
    n_tiles = m // TM
    sample = prev is not None
    if sample:
        shift = TM // dec_seq
        hist = keep = (FFN_CONV - 1) * shift
        tiles_per_seq = None
    else:
        shift = 1
        hist = keep = 8
        tiles_per_seq = seq // TM
    in_specs = [pl.BlockSpec((TM, d), lambda i, j: (i, 0)),
                _layer_spec(g_pre, l, 2),
                pl.BlockSpec((None, d, TF), lambda i, j: (l, 0, j)),
                pl.BlockSpec((None, d, TF), lambda i, j: (l, 0, nf + j)),
                pl.BlockSpec((None, FFN_CONV, TF), lambda i, j: (l, 0, j)),
                pl.BlockSpec((None, 1, TF), lambda i, j: (l, 0, j)),
                pl.BlockSpec((None, TF, d), lambda i, j: (l, j, 0)),
                _layer_spec(g_post, l, 2)]
    args = [x, g_pre, up, up, conv_w, conv_b, down, g_post]
    scratch = [pltpu.VMEM((TM, d), BF16), pltpu.VMEM((TM, d), F32), pltpu.VMEM((hist + TM, TF), F32)]
    if sample:
        in_specs.append(pl.BlockSpec((None, hist, TF), lambda i, j: (l, i, j)))
        args.append(prev)
    else:
        scratch.append(pltpu.VMEM((nf, keep, TF), F32))
    return pl.pallas_call(
        functools.partial(_ffn_kernel, shift, tiles_per_seq, sample),
        out_shape=(jax.ShapeDtypeStruct((m, d), F32),
                   jax.ShapeDtypeStruct((n_tiles * keep, dff), F32)),
        grid=(n_tiles, nf),
        in_specs=in_specs,
        out_specs=(pl.BlockSpec((TM, d), lambda i, j: (i, 0)),
                   pl.BlockSpec((keep, TF), lambda i, j: (i, j))),
        scratch_shapes=scratch,
        compiler_params=_params("arbitrary", "arbitrary"),
        name="ffn_sample" if sample else "ffn_prompt",
    )(*args)


def _to_tiles(a, sb):
    *lead, b, t, c = a.shape
    n = len(lead)
    a = a.reshape(*lead, b // sb, sb, t, c)
    a = jnp.swapaxes(a, n + 1, n + 2)
    return a.reshape(*lead, b * t, c)


def _from_tiles(a, t, sb):
    c = a.shape[-1]
    n = a.shape[0] // (t * sb)
    return a.reshape(n, t, sb, c).transpose(0, 2, 1, 3).reshape(n * sb, t, c)


def kernel(x_prompt, x_sample, state_pool, state_conv, state_ffn, g_mix_pre, g_mix_post, g_ffn_pre,
           g_ffn_post, w_in, pool_w, pool_scale, conv_w, sgu_norm, sgu_w, sgu_b, w_out, ffn_up,
           ffn_conv_w, ffn_conv_b, ffn_down):
    batch, seq, d = x_prompt.shape
    dec_batch, dec_seq, _ = x_sample.shape
    depth = w_in.shape[0]
    pw = pool_scale.shape[1]
    cw = conv_w.shape[2]
    sw = sgu_norm.shape[1]
    widths = (pw, cw, sw)
    assert seq % TM == 0 and TM % dec_seq == 0 and TM % HEAD == 0
    assert PAST_LEN % HEAD == 0 and dec_seq <= HEAD
    sb = TM // dec_seq
    assert dec_batch % sb == 0
    tps = seq // TM

    xp = x_prompt.reshape(batch * seq, d)
    xs = _to_tiles(x_sample, sb)
    st_pool = _to_tiles(state_pool, sb)
    st_conv = _to_tiles(state_conv, sb)
    st_ffn = _to_tiles(state_ffn, sb)

    vec = lambda a: a[:, None, :]
    g_mix_pre, g_mix_post, g_ffn_pre, g_ffn_post = map(vec, (g_mix_pre, g_mix_post, g_ffn_pre, g_ffn_post))
    pool_scale, sgu_norm, ffn_conv_b = map(vec, (pool_scale, sgu_norm, ffn_conv_b))
    w_in_b = w_in.astype(BF16)
    w_out_b = w_out.astype(BF16)
    up_b = ffn_up.astype(BF16)
    pool_w_b = pool_w.astype(BF16)
    sgu_w_b = sgu_w.astype(BF16)
    sgu_bx = jnp.repeat(jnp.swapaxes(sgu_b, 1, 2), HEAD, axis=2)
    sgu_wx = jnp.repeat(jnp.transpose(sgu_w[:, :, :dec_seq, :dec_seq], (0, 2, 3, 1)), HEAD, axis=3)
    sgu_wx = sgu_wx.reshape(depth, dec_seq * dec_seq, sw)

    outs = {k: [] for k in ("pool_p", "pool_s", "conv_p", "conv_s", "ffn_p", "ffn_s", "v_s")}
    for l in range(depth):
        pp = _inproj(l, xp, g_mix_pre, w_in_b)
        ps = _inproj(l, xs, g_mix_pre, w_in_b)
        mix_p, zt = _mix_prompt(l, pp, seq, widths, pool_w_b, pool_scale, conv_w, sgu_norm, sgu_w_b, sgu_bx)
        mix_s, zs, vn = _mix_sample(l, ps, dec_seq, widths, st_pool, st_conv, pool_w_b, pool_scale, conv_w,
                                    sgu_norm, sgu_wx, sgu_bx)
        xp = _outproj(l, mix_p, w_out_b, xp, g_mix_post)
        xs = _outproj(l, mix_s, w_out_b, xs, g_mix_post)
        xp, gt_p = _ffn(l, xp, g_ffn_pre, up_b, ffn_conv_w, ffn_conv_b, ffn_down, g_ffn_post, seq=seq)
        xs, gt_s = _ffn(l, xs, g_ffn_pre, up_b, ffn_conv_w, ffn_conv_b, ffn_down, g_ffn_post,
                        dec_seq=dec_seq, prev=st_ffn)

        xa_s = _from_tiles(ps[:, :pw], dec_seq, sb)
        outs["pool_p"].append(pp.reshape(batch, seq, -1)[:, seq - POOL_HIST:, :pw])
        outs["pool_s"].append(jnp.concatenate([state_pool[l], xa_s], axis=1)[:, -POOL_HIST:])
        outs["conv_p"].append(zt.reshape(batch, tps, 8, cw)[:, -1, 8 - (SHORT_CONV - 1):])
        outs["conv_s"].append(_from_tiles(zs, SHORT_CONV - 1, sb))
        outs["ffn_p"].append(gt_p.reshape(batch, tps, 8, -1)[:, -1, 8 - (FFN_CONV - 1):])
        outs["ffn_s"].append(_from_tiles(gt_s, FFN_CONV - 1, sb))
        outs["v_s"].append(_from_tiles(vn, dec_seq, sb))

    st = lambda k: jnp.stack(outs[k])
    return (xp.reshape(batch, seq, d), _from_tiles(xs, dec_seq, sb), st("pool_p"), st("pool_s"),
            st("conv_p"), st("conv_s"), st("ffn_p"), st("ffn_s"), st("v_s"))
```
